```python
import jax, jax.numpy as jnp
from jax import lax
import numpy as np

D_MODEL = 1024
BATCH = 2
SEQ = 16384
DEPTH = 2

N_MIXERS = 2
N_HEADS = 16
HEAD_DIM = D_MODEL // N_HEADS
MOBA_BLOCK = 256
MOBA_TOPK = 3
Q_CHUNK = 32
ROPE_THETA = 10000.0
CONV_WIDTH = 3
D_FF = 2816
RMS_EPS = 1e-6
NEG_INF = -1e30
N_CONV_LAYERS = (DEPTH + 1) // 2
N_ATTN_LAYERS = DEPTH // 2

kernel_name = "hybrid_shortconv_moba_convffn"


def rmsnorm(x, g):
    xf = x.astype(jnp.float32)
    var = jnp.mean(xf * xf, axis=-1, keepdims=True)
    return (xf * lax.rsqrt(var + RMS_EPS)).astype(x.dtype) * g


def causal_dwconv(x, w):
    S = x.shape[1]
    xp = jnp.pad(x, ((0, 0), (CONV_WIDTH - 1, 0), (0, 0)))
    y = xp[:, 0:S] * w[0]
    for j in range(1, CONV_WIDTH):
        y = y + xp[:, j:j + S] * w[j]
    return y


def short_conv_mixer(h, w_in, w_conv, w_out):
    bcv = h @ w_in
    b, c, v = jnp.split(bcv, 3, axis=-1)
    return (b * causal_dwconv(c * v, w_conv)) @ w_out


def rope(x, pos):
    half = HEAD_DIM // 2
    inv = ROPE_THETA ** (-jnp.arange(half, dtype=jnp.float32) / half)
    ang = pos.astype(jnp.float32)[:, None] * inv[None, :]
    cos = jnp.cos(ang).astype(x.dtype)
    sin = jnp.sin(ang).astype(x.dtype)
    x1, x2 = x[..., :half], x[..., half:]
    return jnp.concatenate([x1 * cos - x2 * sin, x2 * cos + x1 * sin], axis=-1)


def moba_attention(h, w_qkv, w_o):
    Bsz, S, _ = h.shape
    qkv = (h @ w_qkv).reshape(Bsz, S, 3, N_HEADS, HEAD_DIM)
    q = jnp.transpose(qkv[:, :, 0], (0, 2, 1, 3))
    k = jnp.transpose(qkv[:, :, 1], (0, 2, 1, 3))
    v = jnp.transpose(qkv[:, :, 2], (0, 2, 1, 3))
    pos = jnp.arange(S, dtype=jnp.int32)
    q = rope(q, pos) * (HEAD_DIM ** -0.5)
    k = rope(k, pos)

    n_blocks = -(-S // MOBA_BLOCK)
    pad = n_blocks * MOBA_BLOCK - S
    k = jnp.pad(k, ((0, 0), (0, 0), (0, pad), (0, 0)))
    v = jnp.pad(v, ((0, 0), (0, 0), (0, pad), (0, 0)))
    k_blocks = k.reshape(Bsz, N_HEADS, n_blocks, MOBA_BLOCK, HEAD_DIM)
    v_blocks = v.reshape(Bsz, N_HEADS, n_blocks, MOBA_BLOCK, HEAD_DIM)
    k_mean = jnp.mean(k_blocks.astype(jnp.float32), axis=3).astype(k.dtype)

    topk = min(MOBA_TOPK, n_blocks)
    b_idx = jnp.arange(Bsz)[:, None, None, None]
    h_idx = jnp.arange(N_HEADS)[None, :, None, None]
    block_ids = jnp.arange(n_blocks)
    key_off = jnp.arange(MOBA_BLOCK)
    q_off = jnp.arange(Q_CHUNK)

    def chunk(c):
        q0 = c * Q_CHUNK
        qc = lax.dynamic_slice_in_dim(q, q0, Q_CHUNK, axis=2)
        qpos = q0 + q_off
        own = q0 // MOBA_BLOCK
        gate = jnp.einsum('bhqd,bhnd->bhqn', qc, k_mean).astype(jnp.float32)
        gate = jnp.where((block_ids < own)[None, None, None, :], gate, NEG_INF)
        _, top_i = lax.top_k(gate, topk)
        rank_valid = jnp.arange(topk) < jnp.minimum(own, MOBA_TOPK)
        kg = k_blocks[b_idx, h_idx, top_i]
        vg = v_blocks[b_idx, h_idx, top_i]
        s_sel = jnp.einsum('bhqd,bhqnjd->bhqnj', qc, kg).astype(jnp.float32)
        s_sel = jnp.where(rank_valid[:, None], s_sel, NEG_INF)
        s_sel = s_sel.reshape(Bsz, N_HEADS, Q_CHUNK, topk * MOBA_BLOCK)
        ko = lax.dynamic_slice_in_dim(k, own * MOBA_BLOCK, MOBA_BLOCK, axis=2)
        vo = lax.dynamic_slice_in_dim(v, own * MOBA_BLOCK, MOBA_BLOCK, axis=2)
        s_own = jnp.einsum('bhqd,bhjd->bhqj', qc, ko).astype(jnp.float32)
        kpos = own * MOBA_BLOCK + key_off
        s_own = jnp.where(kpos[None, :] <= qpos[:, None], s_own, NEG_INF)
        p = jax.nn.softmax(jnp.concatenate([s_sel, s_own], axis=-1), axis=-1).astype(v.dtype)
        p_sel = p[..., :topk * MOBA_BLOCK].reshape(Bsz, N_HEADS, Q_CHUNK, topk, MOBA_BLOCK)
        p_own = p[..., topk * MOBA_BLOCK:]
        return (jnp.einsum('bhqnj,bhqnjd->bhqd', p_sel, vg)
                + jnp.einsum('bhqj,bhjd->bhqd', p_own, vo))

    o = lax.map(chunk, jnp.arange(S // Q_CHUNK))
    o = jnp.transpose(o, (1, 0, 3, 2, 4)).reshape(Bsz, S, N_HEADS * HEAD_DIM)
    return o @ w_o


def conv_ffn(h, w_up, w_conv, w_down):
    gu = causal_dwconv(h @ w_up, w_conv)
    g, u = jnp.split(gu, 2, axis=-1)
    return (jax.nn.silu(g) * u) @ w_down


def setup_inputs(seed: int = 0) -> dict:
    key = jax.random.key(seed)
    ks = jax.random.split(key, 16)
    D = D_MODEL
    f32 = jnp.float32

    def nrm(k, shape, scale):
        return jax.random.normal(k, shape, dtype=f32) * scale

    return {
        "x": nrm(ks[0], (BATCH, SEQ, D), 1.0),
        "mix_norm": 1.0 + nrm(ks[1], (DEPTH, D), 0.02),
        "sc_w_in": nrm(ks[2], (N_CONV_LAYERS, D, 3 * D), D ** -0.5),
        "sc_w_conv": nrm(ks[3], (N_CONV_LAYERS, CONV_WIDTH, D), CONV_WIDTH ** -0.5),
        "sc_w_out": nrm(ks[4], (N_CONV_LAYERS, D, D), D ** -0.5),
        "moba_w_qkv": nrm(ks[5], (N_ATTN_LAYERS, D, 3 * D), D ** -0.5),
        "moba_w_o": nrm(ks[6], (N_ATTN_LAYERS, D, D), D ** -0.5),
        "ffn_norm": 1.0 + nrm(ks[7], (DEPTH, D), 0.02),
        "ffn_w_up": nrm(ks[8], (DEPTH, D, 2 * D_FF), D ** -0.5),
        "ffn_w_conv": nrm(ks[9], (DEPTH, CONV_WIDTH, 2 * D_FF), CONV_WIDTH ** -0.5),
        "ffn_w_down": nrm(ks[10], (DEPTH, D_FF, D), D_FF ** -0.5),
        "final_norm": 1.0 + nrm(ks[11], (D,), 0.02),
    }


def reference(x, mix_norm, sc_w_in, sc_w_conv, sc_w_out, moba_w_qkv, moba_w_o,
              ffn_norm, ffn_w_up, ffn_w_conv, ffn_w_down, final_norm):
    for i in range(DEPTH):
        h = rmsnorm(x, mix_norm[i])
        j = i // N_MIXERS
        if i % N_MIXERS == 0:
            x = x + short_conv_mixer(h, sc_w_in[j], sc_w_conv[j], sc_w_out[j])
        else:
            x = x + moba_attention(h, moba_w_qkv[j], moba_w_o[j])
        x = x + conv_ffn(rmsnorm(x, ffn_norm[i]), ffn_w_up[i], ffn_w_conv[i], ffn_w_down[i])
    return rmsnorm(x, final_norm)
```

```python
import functools

import jax
import jax.numpy as jnp
from jax import lax
from jax.experimental import pallas as pl
from jax.experimental.pallas import tpu as pltpu

N_HEADS = 16
MOBA_BLOCK = 256
MOBA_TOPK = 3
ROPE_THETA = 10000.0
RMS_EPS = 1e-6
NEG_INF = -1e30
BELOW_NEG_INF = -3.0e38
LOG2E = 1.4426950408889634
CARRY_ROWS = 8
LANES = 128
HEADS_PER_STEP = 2

TM_MIXER = 512
CK_MIXER = 512
TM_FFN = 512
CK_FFN = 256
VMEM_LIMIT = 56 * 1024 * 1024

_BF16 = jnp.bfloat16
_F32 = jnp.float32


def _dot(a, b):
    return jnp.dot(a, b, preferred_element_type=_F32)


def _rmsnorm(x, g):
    var = jnp.mean(x * x, axis=-1, keepdims=True)
    return x * lax.rsqrt(var + RMS_EPS) * g


def _delay_rows(u, prev, n):
    rolled = pltpu.roll(u, n, axis=0)
    tail = pltpu.roll(prev, n, axis=0)
    row = lax.broadcasted_iota(jnp.int32, prev.shape, 0)
    first = jnp.where(row < n, tail, rolled[:CARRY_ROWS])
    return jnp.concatenate([first, rolled[CARRY_ROWS:]], axis=0)


def _causal_conv3(u, prev, w):
    return _delay_rows(u, prev, 2) * w[0:1] + _delay_rows(u, prev, 1) * w[1:2] + u * w[2:3]


def _reset_carry_at_sequence_start(carry_ref):
    @pl.when(pl.program_id(1) == 0)
    def _():
        carry_ref[...] = jnp.zeros_like(carry_ref)


def _mixer_kernel(x_ref, g_ref, win_ref, wconv_ref, wout_ref, o_ref, carry_ref, *, ck):
    _reset_carry_at_sequence_start(carry_ref)
    x = x_ref[0]
    d = x.shape[-1]
    h = _rmsnorm(x, g_ref[...]).astype(_BF16)
    acc = x
    for lo in range(0, d, ck):
        bb = _dot(h, win_ref[:, lo:lo + ck])
        cv = _dot(h, win_ref[:, d + lo:d + lo + ck]) * _dot(h, win_ref[:, 2 * d + lo:2 * d + lo + ck])
        y = _causal_conv3(cv, carry_ref[:, lo:lo + ck], wconv_ref[:, lo:lo + ck])
        carry_ref[:, lo:lo + ck] = cv[-CARRY_ROWS:, :]
        acc = acc + _dot((bb * y).astype(_BF16), wout_ref[lo:lo + ck, :])
    o_ref[0] = acc


def _const_spec(shape):
    zeros = (0,) * len(shape)
    return pl.BlockSpec(shape, lambda *_: zeros, pipeline_mode=pl.Buffered(1))


def _mixer(x, g, w_in, w_conv, w_out):
    bsz, s, d = x.shape
    tm = min(TM_MIXER, s)
    tile = pl.BlockSpec((1, tm, d), lambda b, t: (b, t, 0))
    return pl.pallas_call(
        functools.partial(_mixer_kernel, ck=CK_MIXER),
        grid=(bsz, s // tm),
        in_specs=[tile, _const_spec((1, d)), _const_spec((d, 3 * d)), _const_spec((3, d)), _const_spec((d, d))],
        out_specs=tile,
        out_shape=jax.ShapeDtypeStruct(x.shape, _F32),
        scratch_shapes=[pltpu.VMEM((CARRY_ROWS, d), _F32)],
        compiler_params=pltpu.CompilerParams(
            dimension_semantics=("arbitrary", "arbitrary"), vmem_limit_bytes=VMEM_LIMIT),
        name="mixer",
    )(x, g.reshape(1, d), w_in.astype(_BF16), w_conv, w_out.astype(_BF16))


def _ffn_body(x, g_ref, wup_ref, wconv_ref, wdown_ref, carry_ref, *, d_ff, ck):
    h = _rmsnorm(x, g_ref[...]).astype(_BF16)
    acc = x
    for lo in range(0, d_ff, ck):
        halves = []
        for col in (lo, d_ff + lo):
            up = _dot(h, wup_ref[:, col:col + ck])
            halves.append(_causal_conv3(up, carry_ref[:, col:col + ck], wconv_ref[:, col:col + ck]))
            carry_ref[:, col:col + ck] = up[-CARRY_ROWS:, :]
        gate, lin = halves
        act = gate / (1.0 + jnp.exp(-gate)) * lin
        acc = acc + _dot(act.astype(_BF16), wdown_ref[lo:lo + ck, :])
    return acc


def _ffn_kernel(x_ref, g_ref, wup_ref, wconv_ref, wdown_ref, o_ref, carry_ref, *, d_ff, ck):
    _reset_carry_at_sequence_start(carry_ref)
    o_ref[0] = _ffn_body(x_ref[0], g_ref, wup_ref, wconv_ref, wdown_ref, carry_ref, d_ff=d_ff, ck=ck)


def _proj_ffn_norm_kernel(x_ref, a_ref, wo_ref, g_ref, wup_ref, wconv_ref, wdown_ref, fg_ref, o_ref, carry_ref,
                          *, d_ff, ck):
    _reset_carry_at_sequence_start(carry_ref)
    x = x_ref[0] + _dot(a_ref[0], wo_ref[...])
    y = _ffn_body(x, g_ref, wup_ref, wconv_ref, wdown_ref, carry_ref, d_ff=d_ff, ck=ck)
    o_ref[0] = _rmsnorm(y, fg_ref[...])


def _ffn(x, g, w_up, w_conv, w_down, attn=None, w_o=None, final_g=None):
    bsz, s, d = x.shape
    d_ff = w_down.shape[0]
    tm = min(TM_FFN, s)
    tile = pl.BlockSpec((1, tm, d), lambda b, t: (b, t, 0))
    ffn_specs = [_const_spec((1, d)), _const_spec((d, 2 * d_ff)), _const_spec((3, 2 * d_ff)), _const_spec((d_ff, d))]
    ffn_args = [g.reshape(1, d), w_up.astype(_BF16), w_conv, w_down.astype(_BF16)]
    if attn is None:
        body, name = _ffn_kernel, "ffn"
        in_specs, args = [tile] + ffn_specs, [x] + ffn_args
    else:
        body, name = _proj_ffn_norm_kernel, "proj_ffn_norm"
        in_specs = [tile, tile, _const_spec((d, d))] + ffn_specs + [_const_spec((1, d))]
        args = [x, attn, w_o.astype(_BF16)] + ffn_args + [final_g.reshape(1, d)]
    return pl.pallas_call(
        functools.partial(body, d_ff=d_ff, ck=CK_FFN),
        grid=(bsz, s // tm),
        in_specs=in_specs,
        out_specs=tile,
        out_shape=jax.ShapeDtypeStruct(x.shape, _F32),
        scratch_shapes=[pltpu.VMEM((CARRY_ROWS, 2 * d_ff), _F32)],
        compiler_params=pltpu.CompilerParams(
            dimension_semantics=("arbitrary", "arbitrary"), vmem_limit_bytes=VMEM_LIMIT),
        name=name,
    )(*args)


def _rope_table_kernel(inv_row_ref, sign_row_ref, inv_col_ref, cos_k_ref, sin_k_ref, cos_t_ref, sin_t_ref):
    base = pl.program_id(0) * MOBA_BLOCK
    pos_rows = (base + lax.broadcasted_iota(jnp.int32, cos_k_ref.shape, 0)).astype(_F32)
    ang = pos_rows * inv_row_ref[...]
    cos_k_ref[...] = jnp.cos(ang)
    sin_k_ref[...] = jnp.sin(ang) * sign_row_ref[...]
    pos_cols = (base + lax.broadcasted_iota(jnp.int32, cos_t_ref.shape[1:], 1)).astype(_F32)
    ang_t = pos_cols * inv_col_ref[...]
    cos_t_ref[0] = jnp.cos(ang_t)
    sin_t_ref[0] = jnp.sin(ang_t)


def _rope_tables(s, head_dim):
    half = head_dim // 2
    nb = s // MOBA_BLOCK
    inv = ROPE_THETA ** (-jnp.arange(half, dtype=_F32) / half)
    inv_row = jnp.tile(inv, LANES // half).reshape(1, LANES)
    sign_row = jnp.tile(jnp.concatenate([-jnp.ones(half, _F32), jnp.ones(half, _F32)]),
                        LANES // head_dim).reshape(1, LANES)
    inv_col = jnp.broadcast_to(inv[:, None], (half, MOBA_BLOCK))
    return pl.pallas_call(
        _rope_table_kernel,
        grid=(nb,),
        in_specs=[pl.BlockSpec((1, LANES), lambda t: (0, 0)), pl.BlockSpec((1, LANES), lambda t: (0, 0)),
                  pl.BlockSpec((half, MOBA_BLOCK), lambda t: (0, 0))],
        out_specs=[pl.BlockSpec((MOBA_BLOCK, LANES), lambda t: (t, 0)),
                   pl.BlockSpec((MOBA_BLOCK, LANES), lambda t: (t, 0)),
                   pl.BlockSpec((1, half, MOBA_BLOCK), lambda t: (t, 0, 0)),
                   pl.BlockSpec((1, half, MOBA_BLOCK), lambda t: (t, 0, 0))],
        out_shape=[jax.ShapeDtypeStruct((s, LANES), _F32), jax.ShapeDtypeStruct((s, LANES), _F32),
                   jax.ShapeDtypeStruct((nb, half, MOBA_BLOCK), _F32),
                   jax.ShapeDtypeStruct((nb, half, MOBA_BLOCK), _F32)],
        name="rope_tables",
    )(inv_row, sign_row, inv_col)


def _qkv_kernel(x_ref, g_ref, wqt_ref, wk_ref, wvt_ref, cos_k_ref, sin_k_ref, cos_t_ref, sin_t_ref,
                qt_ref, k_ref, vt_ref, kmean_ref, *, head_dim):
    x = x_ref[0]
    d = x.shape[-1]
    half = head_dim // 2
    h = _rmsnorm(x, g_ref[...]).astype(_BF16)
    nt = (((1,), (1,)), ((), ()))

    qt = lax.dot_general(wqt_ref[...], h, nt, preferred_element_type=_F32)
    cos_t, sin_t = cos_t_ref[0], sin_t_ref[0]
    q_scale = head_dim ** -0.5 * LOG2E
    for lo in range(0, d, head_dim):
        x1, x2 = qt[lo:lo + half], qt[lo + half:lo + head_dim]
        qt_ref[0, 0, lo:lo + half, :] = ((x1 * cos_t - x2 * sin_t) * q_scale).astype(_BF16)
        qt_ref[0, 0, lo + half:lo + head_dim, :] = ((x2 * cos_t + x1 * sin_t) * q_scale).astype(_BF16)

    vt_ref[0, 0] = lax.dot_general(wvt_ref[...], h, nt, preferred_element_type=_F32).astype(_BF16)

    cos_k, sin_k = cos_k_ref[...], sin_k_ref[...]
    lane = lax.broadcasted_iota(jnp.int32, cos_k.shape, 1)
    first_half = (lane % head_dim) < half
    for lo in range(0, d, LANES):
        kk = _dot(h, wk_ref[:, lo:lo + LANES])
        partner = jnp.where(first_half, pltpu.roll(kk, LANES - half, axis=1), pltpu.roll(kk, half, axis=1))
        roped = kk * cos_k + partner * sin_k
        k_ref[0, :, lo:lo + LANES] = roped.astype(_BF16)
        kmean_ref[0, 0, :, lo:lo + LANES] = jnp.mean(roped, axis=0, keepdims=True)


def _qkv(x, g, w_qkv, tables):
    bsz, s, d = x.shape
    head_dim = d // N_HEADS
    half = head_dim // 2
    nb = s // MOBA_BLOCK
    wq, wk, wv = jnp.split(w_qkv.astype(_BF16), 3, axis=1)
    cos_k, sin_k, cos_t, sin_t = tables
    tile = pl.BlockSpec((1, MOBA_BLOCK, d), lambda b, t: (b, t, 0))
    tposed = pl.BlockSpec((1, 1, d, MOBA_BLOCK), lambda b, t: (b, t, 0, 0))
    k_table = pl.BlockSpec((MOBA_BLOCK, LANES), lambda b, t: (t, 0))
    t_table = pl.BlockSpec((1, half, MOBA_BLOCK), lambda b, t: (t, 0, 0))
    return pl.pallas_call(
        functools.partial(_qkv_kernel, head_dim=head_dim),
        grid=(bsz, nb),
        in_specs=[tile, _const_spec((1, d)), _const_spec((d, d)), _const_spec((d, d)), _const_spec((d, d)),
                  k_table, k_table, t_table, t_table],
        out_specs=[tposed, tile, tposed, pl.BlockSpec((1, 1, 1, d), lambda b, t: (b, t, 0, 0))],
        out_shape=[jax.ShapeDtypeStruct((bsz, nb, d, MOBA_BLOCK), _BF16),
                   jax.ShapeDtypeStruct((bsz, s, d), _BF16),
                   jax.ShapeDtypeStruct((bsz, nb, d, MOBA_BLOCK), _BF16),
                   jax.ShapeDtypeStruct((bsz, nb, 1, d), _F32)],
        compiler_params=pltpu.CompilerParams(
            dimension_semantics=("arbitrary", "arbitrary"), vmem_limit_bytes=VMEM_LIMIT),
        name="qkv_rope",
    )(x, g.reshape(1, d), wq.T, wk, wv.T, cos_k, sin_k, cos_t, sin_t)


def _moba_kernel(qt_ref, k_ref, vt_ref, kmean_ref, o_ref, bias_ref, *, head_dim):
    own = pl.program_id(2)
    nb = kmean_ref.shape[1]
    blk = MOBA_BLOCK
    qt_pair = qt_ref[0, 0]
    kmean = kmean_ref[0].astype(_BF16)
    feat = lax.broadcasted_iota(jnp.int32, qt_pair.shape, 0)
    block_id = lax.broadcasted_iota(jnp.int32, (nb, blk), 0)
    key_pos = lax.broadcasted_iota(jnp.int32, (blk, blk), 0)
    query_pos = lax.broadcasted_iota(jnp.int32, (blk, blk), 1)
    k_own = k_ref[0, pl.ds(pl.multiple_of(own * blk, blk), blk), :]

    state = []
    q_heads = []
    for hh in range(HEADS_PER_STEP):
        rows = slice(hh * head_dim, (hh + 1) * head_dim)
        q_h = jnp.where((feat >= hh * head_dim) & (feat < (hh + 1) * head_dim), qt_pair, jnp.zeros_like(qt_pair))
        q_heads.append(q_h)

        gate = jnp.where(block_id < own, _dot(kmean, q_h), NEG_INF)
        bias = jnp.full((nb, blk), NEG_INF, _F32)
        for _ in range(MOBA_TOPK):
            best = jnp.max(gate, axis=0, keepdims=True)
            first = jnp.min(jnp.where(gate == best, block_id, nb), axis=0, keepdims=True)
            picked = block_id == first
            bias = jnp.where(picked, 0.0, bias)
            gate = jnp.where(picked, BELOW_NEG_INF, gate)
        bias_ref[hh] = jnp.where(block_id < own, bias, NEG_INF)

        s = jnp.where(key_pos <= query_pos, _dot(k_own, q_h), NEG_INF)
        m = jnp.max(s, axis=0, keepdims=True)
        p = jnp.exp2(s - m)
        l = jnp.sum(p, axis=0, keepdims=True)
        acc = _dot(vt_ref[0, own, rows, :], p.astype(_BF16))
        state += [m, l, acc]

    def visit(j, carry):
        k_j = k_ref[0, pl.ds(pl.multiple_of(j * blk, blk), blk), :]
        out = []
        for hh in range(HEADS_PER_STEP):
            m, l, acc = carry[3 * hh:3 * hh + 3]
            s = _dot(k_j, q_heads[hh]) + bias_ref[hh, pl.ds(j, 1), :]
            m_new = jnp.maximum(m, jnp.max(s, axis=0, keepdims=True))
            alpha = jnp.exp2(m - m_new)
            p = jnp.exp2(s - m_new)
            l = alpha * l + jnp.sum(p, axis=0, keepdims=True)
            v_j = vt_ref[0, j, hh * head_dim:(hh + 1) * head_dim, :]
            out += [m_new, l, alpha * acc + _dot(v_j, p.astype(_BF16))]
        return tuple(out)

    state = lax.fori_loop(0, own, visit, tuple(state))
    o_t = jnp.concatenate([state[3 * hh + 2] / state[3 * hh + 1] for hh in range(HEADS_PER_STEP)], axis=0)
    o_ref[0] = o_t.T.astype(_BF16)


def _moba(qt, k, vt, kmean):
    bsz, s, d = k.shape
    head_dim = d // N_HEADS
    nb = s // MOBA_BLOCK
    width = HEADS_PER_STEP * head_dim
    return pl.pallas_call(
        functools.partial(_moba_kernel, head_dim=head_dim),
        grid=(bsz, d // width, nb),
        in_specs=[pl.BlockSpec((1, 1, width, MOBA_BLOCK), lambda b, c, i: (b, i, c, 0)),
                  pl.BlockSpec((1, s, width), lambda b, c, i: (b, 0, c)),
                  pl.BlockSpec((1, nb, width, MOBA_BLOCK), lambda b, c, i: (b, 0, c, 0)),
                  pl.BlockSpec((1, nb, width), lambda b, c, i: (b, 0, c))],
        out_specs=pl.BlockSpec((1, MOBA_BLOCK, width), lambda b, c, i: (b, i, c)),
        out_shape=jax.ShapeDtypeStruct((bsz, s, d), _BF16),
        scratch_shapes=[pltpu.VMEM((HEADS_PER_STEP, nb, MOBA_BLOCK), _F32)],
        compiler_params=pltpu.CompilerParams(
            dimension_semantics=("arbitrary", "arbitrary", "arbitrary"), vmem_limit_bytes=VMEM_LIMIT),
        name="moba_attention",
    )(qt, k, vt, kmean.reshape(bsz, nb, d))


def kernel(x, mix_norm, sc_w_in, sc_w_conv, sc_w_out, moba_w_qkv, moba_w_o, ffn_norm, ffn_w_up, ffn_w_conv,
           ffn_w_down, final_norm):
    bsz, s, d = x.shape
    assert d % (N_HEADS * HEADS_PER_STEP) == 0 and d // N_HEADS * HEADS_PER_STEP == LANES
    assert s % MOBA_BLOCK == 0 and s % min(TM_FFN, s) == 0 and s % min(TM_MIXER, s) == 0
    assert mix_norm.shape[0] == 2 and ffn_w_down.shape[1] % CK_FFN == 0 and d % CK_MIXER == 0

    x = _mixer(x, mix_norm[0], sc_w_in[0], sc_w_conv[0], sc_w_out[0])
    x = _ffn(x, ffn_norm[0], ffn_w_up[0], ffn_w_conv[0], ffn_w_down[0])
    qt, k, vt, kmean = _qkv(x, mix_norm[1], moba_w_qkv[0], _rope_tables(s, d // N_HEADS))
    attn = _moba(qt, k, vt, kmean)
    return _ffn(x, ffn_norm[1], ffn_w_up[1], ffn_w_conv[1], ffn_w_down[1],
                attn=attn, w_o=moba_w_o[0], final_g=final_norm)
```

```python
import functools

import jax
import jax.numpy as jnp
from jax import lax
from jax.experimental import pallas as pl
from jax.experimental.pallas import tpu as pltpu

N_HEADS = 16
MOBA_BLOCK = 256
MOBA_TOPK = 3
ROPE_THETA = 10000.0
RMS_EPS = 1e-6
NEG_INF = -1e30
BELOW_NEG_INF = -3.0e38
LOG2E = 1.4426950408889634
CARRY_ROWS = 8
LANES = 128
HEADS_PER_STEP = 2
ONES_ROWS = 16
BLOCKS_PER_VISIT = 8
OVERFLOW_GUARD = 64.0

TM_MIXER = 512
CK_MIXER = 512
TM_FFN = 512
CK_FFN = 256
VMEM_LIMIT = 56 * 1024 * 1024

_BF16 = jnp.bfloat16
_F32 = jnp.float32


def _dot(a, b):
    return jnp.dot(a, b, preferred_element_type=_F32)


def _rmsnorm(x, g):
    var = jnp.mean(x * x, axis=-1, keepdims=True)
    return x * lax.rsqrt(var + RMS_EPS) * g


def _delay_rows(u, prev, n):
    rolled = pltpu.roll(u, n, axis=0)
    tail = pltpu.roll(prev, n, axis=0)
    row = lax.broadcasted_iota(jnp.int32, prev.shape, 0)
    first = jnp.where(row < n, tail, rolled[:CARRY_ROWS])
    return jnp.concatenate([first, rolled[CARRY_ROWS:]], axis=0)


def _causal_conv3(u, prev, w):
    return _delay_rows(u, prev, 2) * w[0:1] + _delay_rows(u, prev, 1) * w[1:2] + u * w[2:3]


def _reset_carry_at_sequence_start(carry_ref):
    @pl.when(pl.program_id(1) == 0)
    def _():
        carry_ref[...] = jnp.zeros_like(carry_ref)


def _mixer_kernel(x_ref, g_ref, win_ref, wconv_ref, wout_ref, o_ref, carry_ref, *, ck):
    _reset_carry_at_sequence_start(carry_ref)
    x = x_ref[0]
    d = x.shape[-1]
    h = _rmsnorm(x, g_ref[...]).astype(_BF16)
    acc = x
    for lo in range(0, d, ck):
        bb = _dot(h, win_ref[:, lo:lo + ck])
        cv = _dot(h, win_ref[:, d + lo:d + lo + ck]) * _dot(h, win_ref[:, 2 * d + lo:2 * d + lo + ck])
        y = _causal_conv3(cv, carry_ref[:, lo:lo + ck], wconv_ref[:, lo:lo + ck])
        carry_ref[:, lo:lo + ck] = cv[-CARRY_ROWS:, :]
        acc = acc + _dot((bb * y).astype(_BF16), wout_ref[lo:lo + ck, :])
    o_ref[0] = acc


def _const_spec(shape):
    zeros = (0,) * len(shape)
    return pl.BlockSpec(shape, lambda *_: zeros, pipeline_mode=pl.Buffered(1))


def _mixer(x, g, w_in, w_conv, w_out):
    bsz, s, d = x.shape
    tm = min(TM_MIXER, s)
    tile = pl.BlockSpec((1, tm, d), lambda b, t: (b, t, 0))
    return pl.pallas_call(
        functools.partial(_mixer_kernel, ck=CK_MIXER),
        grid=(bsz, s // tm),
        in_specs=[tile, _const_spec((1, d)), _const_spec((d, 3 * d)), _const_spec((3, d)), _const_spec((d, d))],
        out_specs=tile,
        out_shape=jax.ShapeDtypeStruct(x.shape, _F32),
        scratch_shapes=[pltpu.VMEM((CARRY_ROWS, d), _F32)],
        compiler_params=pltpu.CompilerParams(
            dimension_semantics=("arbitrary", "arbitrary"), vmem_limit_bytes=VMEM_LIMIT),
        name="mixer",
    )(x, g.reshape(1, d), w_in.astype(_BF16), w_conv, w_out.astype(_BF16))


def _ffn_body(x, g_ref, wup_ref, wconv_ref, wdown_ref, carry_ref, *, d_ff, ck):
    h = _rmsnorm(x, g_ref[...]).astype(_BF16)
    acc = x
    for lo in range(0, d_ff, ck):
        halves = []
        for col in (lo, d_ff + lo):
            up = _dot(h, wup_ref[:, col:col + ck])
            halves.append(_causal_conv3(up, carry_ref[:, col:col + ck], wconv_ref[:, col:col + ck]))
            carry_ref[:, col:col + ck] = up[-CARRY_ROWS:, :]
        gate, lin = halves
        act = gate / (1.0 + jnp.exp(-gate)) * lin
        acc = acc + _dot(act.astype(_BF16), wdown_ref[lo:lo + ck, :])
    return acc


def _ffn_kernel(x_ref, g_ref, wup_ref, wconv_ref, wdown_ref, o_ref, carry_ref, *, d_ff, ck):
    _reset_carry_at_sequence_start(carry_ref)
    o_ref[0] = _ffn_body(x_ref[0], g_ref, wup_ref, wconv_ref, wdown_ref, carry_ref, d_ff=d_ff, ck=ck)


def _proj_ffn_norm_kernel(x_ref, a_ref, wo_ref, g_ref, wup_ref, wconv_ref, wdown_ref, fg_ref, o_ref, carry_ref,
                          *, d_ff, ck):
    _reset_carry_at_sequence_start(carry_ref)
    x = x_ref[0] + _dot(a_ref[0], wo_ref[...])
    y = _ffn_body(x, g_ref, wup_ref, wconv_ref, wdown_ref, carry_ref, d_ff=d_ff, ck=ck)
    o_ref[0] = _rmsnorm(y, fg_ref[...])


def _ffn(x, g, w_up, w_conv, w_down, attn=None, w_o=None, final_g=None):
    bsz, s, d = x.shape
    d_ff = w_down.shape[0]
    tm = min(TM_FFN, s)
    tile = pl.BlockSpec((1, tm, d), lambda b, t: (b, t, 0))
    ffn_specs = [_const_spec((1, d)), _const_spec((d, 2 * d_ff)), _const_spec((3, 2 * d_ff)), _const_spec((d_ff, d))]
    ffn_args = [g.reshape(1, d), w_up.astype(_BF16), w_conv, w_down.astype(_BF16)]
    if attn is None:
        body, name = _ffn_kernel, "ffn"
        in_specs, args = [tile] + ffn_specs, [x] + ffn_args
    else:
        body, name = _proj_ffn_norm_kernel, "proj_ffn_norm"
        in_specs = [tile, tile, _const_spec((d, d))] + ffn_specs + [_const_spec((1, d))]
        args = [x, attn, w_o.astype(_BF16)] + ffn_args + [final_g.reshape(1, d)]
    return pl.pallas_call(
        functools.partial(body, d_ff=d_ff, ck=CK_FFN),
        grid=(bsz, s // tm),
        in_specs=in_specs,
        out_specs=tile,
        out_shape=jax.ShapeDtypeStruct(x.shape, _F32),
        scratch_shapes=[pltpu.VMEM((CARRY_ROWS, 2 * d_ff), _F32)],
        compiler_params=pltpu.CompilerParams(
            dimension_semantics=("arbitrary", "arbitrary"), vmem_limit_bytes=VMEM_LIMIT),
        name=name,
    )(*args)


def _rope_table_kernel(inv_row_ref, sign_row_ref, inv_col_ref, cos_k_ref, sin_k_ref, cos_t_ref, sin_t_ref):
    base = pl.program_id(0) * MOBA_BLOCK
    pos_rows = (base + lax.broadcasted_iota(jnp.int32, cos_k_ref.shape, 0)).astype(_F32)
    ang = pos_rows * inv_row_ref[...]
    cos_k_ref[...] = jnp.cos(ang)
    sin_k_ref[...] = jnp.sin(ang) * sign_row_ref[...]
    pos_cols = (base + lax.broadcasted_iota(jnp.int32, cos_t_ref.shape[1:], 1)).astype(_F32)
    ang_t = pos_cols * inv_col_ref[...]
    cos_t_ref[0] = jnp.cos(ang_t)
    sin_t_ref[0] = jnp.sin(ang_t)


def _rope_tables(s, head_dim):
    half = head_dim // 2
    nb = s // MOBA_BLOCK
    inv = ROPE_THETA ** (-jnp.arange(half, dtype=_F32) / half)
    inv_row = jnp.tile(inv, LANES // half).reshape(1, LANES)
    sign_row = jnp.tile(jnp.concatenate([-jnp.ones(half, _F32), jnp.ones(half, _F32)]),
                        LANES // head_dim).reshape(1, LANES)
    inv_col = jnp.broadcast_to(inv[:, None], (half, MOBA_BLOCK))
    return pl.pallas_call(
        _rope_table_kernel,
        grid=(nb,),
        in_specs=[pl.BlockSpec((1, LANES), lambda t: (0, 0)), pl.BlockSpec((1, LANES), lambda t: (0, 0)),
                  pl.BlockSpec((half, MOBA_BLOCK), lambda t: (0, 0))],
        out_specs=[pl.BlockSpec((MOBA_BLOCK, LANES), lambda t: (t, 0)),
                   pl.BlockSpec((MOBA_BLOCK, LANES), lambda t: (t, 0)),
                   pl.BlockSpec((1, half, MOBA_BLOCK), lambda t: (t, 0, 0)),
                   pl.BlockSpec((1, half, MOBA_BLOCK), lambda t: (t, 0, 0))],
        out_shape=[jax.ShapeDtypeStruct((s, LANES), _F32), jax.ShapeDtypeStruct((s, LANES), _F32),
                   jax.ShapeDtypeStruct((nb, half, MOBA_BLOCK), _F32),
                   jax.ShapeDtypeStruct((nb, half, MOBA_BLOCK), _F32)],
        name="rope_tables",
    )(inv_row, sign_row, inv_col)


def _qkv_kernel(x_ref, g_ref, wqt_ref, wk_ref, wvt_ref, cos_k_ref, sin_k_ref, cos_t_ref, sin_t_ref,
                qt_ref, k_ref, vt_ref, kmean_ref, *, head_dim):
    x = x_ref[0]
    d = x.shape[-1]
    half = head_dim // 2
    h = _rmsnorm(x, g_ref[...]).astype(_BF16)
    nt = (((1,), (1,)), ((), ()))

    qt = lax.dot_general(wqt_ref[...], h, nt, preferred_element_type=_F32)
    cos_t, sin_t = cos_t_ref[0], sin_t_ref[0]
    q_scale = head_dim ** -0.5 * LOG2E
    for lo in range(0, d, head_dim):
        x1, x2 = qt[lo:lo + half], qt[lo + half:lo + head_dim]
        qt_ref[0, 0, lo:lo + half, :] = ((x1 * cos_t - x2 * sin_t) * q_scale).astype(_BF16)
        qt_ref[0, 0, lo + half:lo + head_dim, :] = ((x2 * cos_t + x1 * sin_t) * q_scale).astype(_BF16)

    vt = lax.dot_general(wvt_ref[...], h, nt, preferred_element_type=_F32).astype(_BF16)
    v_rows = head_dim + ONES_ROWS
    for hd in range(d // head_dim):
        vt_ref[0, 0, hd * v_rows:hd * v_rows + head_dim, :] = vt[hd * head_dim:(hd + 1) * head_dim]
        vt_ref[0, 0, hd * v_rows + head_dim:(hd + 1) * v_rows, :] = jnp.ones((ONES_ROWS, MOBA_BLOCK), _BF16)

    cos_k, sin_k = cos_k_ref[...], sin_k_ref[...]
    lane = lax.broadcasted_iota(jnp.int32, cos_k.shape, 1)
    first_half = (lane % head_dim) < half
    for lo in range(0, d, LANES):
        kk = _dot(h, wk_ref[:, lo:lo + LANES])
        partner = jnp.where(first_half, pltpu.roll(kk, LANES - half, axis=1), pltpu.roll(kk, half, axis=1))
        roped = kk * cos_k + partner * sin_k
        k_ref[0, :, lo:lo + LANES] = roped.astype(_BF16)
        kmean_ref[0, 0, :, lo:lo + LANES] = jnp.mean(roped, axis=0, keepdims=True)


def _qkv(x, g, w_qkv, tables):
    bsz, s, d = x.shape
    head_dim = d // N_HEADS
    half = head_dim // 2
    nb = s // MOBA_BLOCK
    wq, wk, wv = jnp.split(w_qkv.astype(_BF16), 3, axis=1)
    cos_k, sin_k, cos_t, sin_t = tables
    tile = pl.BlockSpec((1, MOBA_BLOCK, d), lambda b, t: (b, t, 0))
    tposed = pl.BlockSpec((1, 1, d, MOBA_BLOCK), lambda b, t: (b, t, 0, 0))
    d_aug = N_HEADS * (head_dim + ONES_ROWS)
    v_tposed = pl.BlockSpec((1, 1, d_aug, MOBA_BLOCK), lambda b, t: (b, t, 0, 0))
    k_table = pl.BlockSpec((MOBA_BLOCK, LANES), lambda b, t: (t, 0))
    t_table = pl.BlockSpec((1, half, MOBA_BLOCK), lambda b, t: (t, 0, 0))
    return pl.pallas_call(
        functools.partial(_qkv_kernel, head_dim=head_dim),
        grid=(bsz, nb),
        in_specs=[tile, _const_spec((1, d)), _const_spec((d, d)), _const_spec((d, d)), _const_spec((d, d)),
                  k_table, k_table, t_table, t_table],
        out_specs=[tposed, tile, v_tposed, pl.BlockSpec((1, 1, 1, d), lambda b, t: (b, t, 0, 0))],
        out_shape=[jax.ShapeDtypeStruct((bsz, nb, d, MOBA_BLOCK), _BF16),
                   jax.ShapeDtypeStruct((bsz, s, d), _BF16),
                   jax.ShapeDtypeStruct((bsz, nb, d_aug, MOBA_BLOCK), _BF16),
                   jax.ShapeDtypeStruct((bsz, nb, 1, d), _F32)],
        compiler_params=pltpu.CompilerParams(
            dimension_semantics=("arbitrary", "arbitrary"), vmem_limit_bytes=VMEM_LIMIT),
        name="qkv_rope",
    )(x, g.reshape(1, d), wq.T, wk, wv.T, cos_k, sin_k, cos_t, sin_t)


def _moba_kernel(qt_ref, k_ref, vt_ref, kmean_ref, o_ref, q_ref, bias_ref, acc_ref, *, head_dim):
    own = pl.program_id(2)
    nb = kmean_ref.shape[1]
    blk = MOBA_BLOCK
    qt_pair = qt_ref[0, 0]
    kmean = kmean_ref[0].astype(_BF16)
    feat = lax.broadcasted_iota(jnp.int32, qt_pair.shape, 0)
    block_id = lax.broadcasted_iota(jnp.int32, (nb, blk), 0)
    key_pos = lax.broadcasted_iota(jnp.int32, (blk, blk), 0)
    query_pos = lax.broadcasted_iota(jnp.int32, (blk, blk), 1)
    k_own = k_ref[0, pl.ds(pl.multiple_of(own * blk, blk), blk), :]

    v_rows = head_dim + ONES_ROWS
    causal = key_pos <= query_pos

    def k_block(j):
        return k_ref[0, pl.ds(pl.multiple_of(j * blk, blk), blk), :]

    def v_block(j, hh):
        return vt_ref[0, j, hh * v_rows:(hh + 1) * v_rows, :]

    q_heads, stabilisers = [], []
    for hh in range(HEADS_PER_STEP):
        q_h = jnp.where((feat >= hh * head_dim) & (feat < (hh + 1) * head_dim), qt_pair, jnp.zeros_like(qt_pair))
        q_heads.append(q_h)
        cols = slice(hh * blk, (hh + 1) * blk)
        q_ref[:, cols] = q_h

        gate = jnp.where(block_id < own, _dot(kmean, q_h), NEG_INF)
        bias = jnp.full((nb, blk), NEG_INF, _F32)
        for _ in range(MOBA_TOPK):
            best = jnp.max(gate, axis=0, keepdims=True)
            first = jnp.min(jnp.where(gate == best, block_id, nb), axis=0, keepdims=True)
            picked = block_id == first
            bias = jnp.where(picked, 0.0, bias)
            gate = jnp.where(picked, BELOW_NEG_INF, gate)

        s = jnp.where(causal, _dot(k_own, q_h), NEG_INF)
        m = jnp.max(s, axis=0, keepdims=True)
        stabilisers.append(m)
        bias_ref[:, cols] = jnp.where(block_id < own, bias, NEG_INF) - m
        acc_ref[hh] = _dot(v_block(own, hh), jnp.exp2(s - m).astype(_BF16))

    def visit_group(i, peak):
        j0 = i * BLOCKS_PER_VISIT
        keys = k_ref[0, pl.ds(pl.multiple_of(j0 * blk, blk), BLOCKS_PER_VISIT * blk), :]
        s = _dot(keys, q_ref[...])
        updates = [None] * HEADS_PER_STEP
        for g in range(BLOCKS_PER_VISIT):
            t = s[g * blk:(g + 1) * blk] + bias_ref[pl.ds(j0 + g, 1), :]
            peak = jnp.maximum(peak, jnp.max(t.reshape(blk // 8, 8, HEADS_PER_STEP * blk), axis=0))
            p = jnp.exp2(t).astype(_BF16)
            for hh in range(HEADS_PER_STEP):
                u = _dot(v_block(j0 + g, hh), p[:, hh * blk:(hh + 1) * blk])
                updates[hh] = u if updates[hh] is None else updates[hh] + u
        for hh in range(HEADS_PER_STEP):
            acc_ref[hh] += updates[hh]
        return peak

    n_groups = (own + BLOCKS_PER_VISIT - 1) // BLOCKS_PER_VISIT
    peak = lax.fori_loop(0, n_groups, visit_group, jnp.full((8, HEADS_PER_STEP * blk), NEG_INF, _F32))
    overflow_risk = jnp.max(peak) > OVERFLOW_GUARD

    @pl.when(overflow_risk)
    def _():
        for hh in range(HEADS_PER_STEP):
            s = jnp.where(causal, _dot(k_own, q_heads[hh]), NEG_INF) - stabilisers[hh]
            acc = _dot(v_block(own, hh), jnp.exp2(s).astype(_BF16))

            def visit_exact(j, carry, hh=hh):
                m_run, acc = carry
                t = _dot(k_block(j), q_heads[hh]) + bias_ref[pl.ds(j, 1), hh * blk:(hh + 1) * blk]
                m_new = jnp.maximum(m_run, jnp.max(t, axis=0, keepdims=True))
                acc = jnp.exp2(m_run - m_new) * acc + _dot(v_block(j, hh), jnp.exp2(t - m_new).astype(_BF16))
                return m_new, acc

            _, acc = lax.fori_loop(0, own, visit_exact, (jnp.zeros((1, blk), _F32), acc))
            acc_ref[hh] = acc

    o_t = jnp.concatenate([acc_ref[hh, :head_dim] / acc_ref[hh, head_dim:head_dim + 1]
                           for hh in range(HEADS_PER_STEP)], axis=0)
    o_ref[0] = o_t.T.astype(_BF16)


def _moba(qt, k, vt, kmean):
    bsz, s, d = k.shape
    head_dim = d // N_HEADS
    nb = s // MOBA_BLOCK
    width = HEADS_PER_STEP * head_dim
    v_rows = head_dim + ONES_ROWS
    return pl.pallas_call(
        functools.partial(_moba_kernel, head_dim=head_dim),
        grid=(bsz, d // width, nb),
        in_specs=[pl.BlockSpec((1, 1, width, MOBA_BLOCK), lambda b, c, i: (b, i, c, 0)),
                  pl.BlockSpec((1, s, width), lambda b, c, i: (b, 0, c)),
                  pl.BlockSpec((1, nb, HEADS_PER_STEP * v_rows, MOBA_BLOCK), lambda b, c, i: (b, 0, c, 0)),
                  pl.BlockSpec((1, nb, width), lambda b, c, i: (b, 0, c))],
        out_specs=pl.BlockSpec((1, MOBA_BLOCK, width), lambda b, c, i: (b, i, c)),
        out_shape=jax.ShapeDtypeStruct((bsz, s, d), _BF16),
        scratch_shapes=[pltpu.VMEM((width, HEADS_PER_STEP * MOBA_BLOCK), _BF16),
                        pltpu.VMEM((nb, HEADS_PER_STEP * MOBA_BLOCK), _F32),
                        pltpu.VMEM((HEADS_PER_STEP, v_rows, MOBA_BLOCK), _F32)],
        compiler_params=pltpu.CompilerParams(
            dimension_semantics=("arbitrary", "arbitrary", "arbitrary"), vmem_limit_bytes=VMEM_LIMIT),
        name="moba_attention",
    )(qt, k, vt, kmean.reshape(bsz, nb, d))


def kernel(x, mix_norm, sc_w_in, sc_w_conv, sc_w_out, moba_w_qkv, moba_w_o, ffn_norm, ffn_w_up, ffn_w_conv,
           ffn_w_down, final_norm):
    bsz, s, d = x.shape
    assert d % (N_HEADS * HEADS_PER_STEP) == 0 and d // N_HEADS * HEADS_PER_STEP == LANES
    assert s % (MOBA_BLOCK * BLOCKS_PER_VISIT) == 0 and s % min(TM_FFN, s) == 0 and s % min(TM_MIXER, s) == 0
    assert mix_norm.shape[0] == 2 and ffn_w_down.shape[1] % CK_FFN == 0 and d % CK_MIXER == 0

    x = _mixer(x, mix_norm[0], sc_w_in[0], sc_w_conv[0], sc_w_out[0])
    x = _ffn(x, ffn_norm[0], ffn_w_up[0], ffn_w_conv[0], ffn_w_down[0])
    qt, k, vt, kmean = _qkv(x, mix_norm[1], moba_w_qkv[0], _rope_tables(s, d // N_HEADS))
    attn = _moba(qt, k, vt, kmean)
    return _ffn(x, ffn_norm[1], ffn_w_up[1], ffn_w_conv[1], ffn_w_down[1],
                attn=attn, w_o=moba_w_o[0], final_g=final_norm)
```

```python
import functools

import jax
import jax.numpy as jnp
from jax import lax
from jax.experimental import pallas as pl
from jax.experimental.pallas import tpu as pltpu

N_HEADS = 16
MOBA_BLOCK = 256
MOBA_TOPK = 3
ROPE_THETA = 10000.0
RMS_EPS = 1e-6
NEG_INF = -1e30
BELOW_NEG_INF = -3.0e38
LOG2E = 1.4426950408889634
CARRY_ROWS = 8
LANES = 128
HEADS_PER_STEP = 4
ONES_ROWS = 16
BLOCKS_PER_VISIT = 8
OVERFLOW_GUARD = 64.0

TM_MIXER = 512
CK_MIXER = 512
TM_FFN = 512
CK_FFN = 256
VMEM_LIMIT = 56 * 1024 * 1024

_BF16 = jnp.bfloat16
_F32 = jnp.float32


def _dot(a, b):
    return jnp.dot(a, b, preferred_element_type=_F32)


def _rmsnorm(x, g):
    var = jnp.mean(x * x, axis=-1, keepdims=True)
    return x * lax.rsqrt(var + RMS_EPS) * g


def _delay_rows(u, prev, n):
    rolled = pltpu.roll(u, n, axis=0)
    tail = pltpu.roll(prev, n, axis=0)
    row = lax.broadcasted_iota(jnp.int32, prev.shape, 0)
    first = jnp.where(row < n, tail, rolled[:CARRY_ROWS])
    return jnp.concatenate([first, rolled[CARRY_ROWS:]], axis=0)


def _causal_conv3(u, prev, w):
    return _delay_rows(u, prev, 2) * w[0:1] + _delay_rows(u, prev, 1) * w[1:2] + u * w[2:3]


def _reset_carry_at_sequence_start(carry_ref):
    @pl.when(pl.program_id(1) == 0)
    def _():
        carry_ref[...] = jnp.zeros_like(carry_ref)


def _mixer_kernel(x_ref, g_ref, win_ref, wconv_ref, wout_ref, o_ref, carry_ref, *, ck):
    _reset_carry_at_sequence_start(carry_ref)
    x = x_ref[0]
    d = x.shape[-1]
    h = _rmsnorm(x, g_ref[...]).astype(_BF16)
    acc = x
    for lo in range(0, d, ck):
        bb = _dot(h, win_ref[:, lo:lo + ck])
        cv = _dot(h, win_ref[:, d + lo:d + lo + ck]) * _dot(h, win_ref[:, 2 * d + lo:2 * d + lo + ck])
        y = _causal_conv3(cv, carry_ref[:, lo:lo + ck], wconv_ref[:, lo:lo + ck])
        carry_ref[:, lo:lo + ck] = cv[-CARRY_ROWS:, :]
        acc = acc + _dot((bb * y).astype(_BF16), wout_ref[lo:lo + ck, :])
    o_ref[0] = acc


def _const_spec(shape):
    zeros = (0,) * len(shape)
    return pl.BlockSpec(shape, lambda *_: zeros, pipeline_mode=pl.Buffered(1))


def _mixer(x, g, w_in, w_conv, w_out):
    bsz, s, d = x.shape
    tm = min(TM_MIXER, s)
    tile = pl.BlockSpec((1, tm, d), lambda b, t: (b, t, 0))
    return pl.pallas_call(
        functools.partial(_mixer_kernel, ck=CK_MIXER),
        grid=(bsz, s // tm),
        in_specs=[tile, _const_spec((1, d)), _const_spec((d, 3 * d)), _const_spec((3, d)), _const_spec((d, d))],
        out_specs=tile,
        out_shape=jax.ShapeDtypeStruct(x.shape, _F32),
        scratch_shapes=[pltpu.VMEM((CARRY_ROWS, d), _F32)],
        compiler_params=pltpu.CompilerParams(
            dimension_semantics=("arbitrary", "arbitrary"), vmem_limit_bytes=VMEM_LIMIT),
        name="mixer",
    )(x, g.reshape(1, d), w_in.astype(_BF16), w_conv, w_out.astype(_BF16))


def _ffn_body(x, g_ref, wup_ref, wconv_ref, wdown_ref, carry_ref, *, d_ff, ck):
    h = _rmsnorm(x, g_ref[...]).astype(_BF16)

    def up_project(lo):
        halves = []
        for col in (lo, d_ff + lo):
            up = _dot(h, wup_ref[:, col:col + ck])
            halves.append((up, carry_ref[:, col:col + ck]))
            carry_ref[:, col:col + ck] = up[-CARRY_ROWS:, :]
        return halves

    chunks = list(range(0, d_ff, ck))
    acc = x
    ahead = up_project(chunks[0])
    for c, lo in enumerate(chunks):
        gate, lin = (_causal_conv3(up, prev, wconv_ref[:, col:col + ck])
                     for (up, prev), col in zip(ahead, (lo, d_ff + lo)))
        act = (gate / (1.0 + jnp.exp(-gate)) * lin).astype(_BF16)
        if c + 1 < len(chunks):
            ahead = up_project(chunks[c + 1])
        acc = acc + _dot(act, wdown_ref[lo:lo + ck, :])
    return acc


def _ffn_kernel(x_ref, g_ref, wup_ref, wconv_ref, wdown_ref, o_ref, carry_ref, *, d_ff, ck):
    _reset_carry_at_sequence_start(carry_ref)
    o_ref[0] = _ffn_body(x_ref[0], g_ref, wup_ref, wconv_ref, wdown_ref, carry_ref, d_ff=d_ff, ck=ck)


def _proj_ffn_norm_kernel(x_ref, a_ref, wo_ref, g_ref, wup_ref, wconv_ref, wdown_ref, fg_ref, o_ref, carry_ref,
                          *, d_ff, ck):
    _reset_carry_at_sequence_start(carry_ref)
    x = x_ref[0] + _dot(a_ref[0], wo_ref[...])
    y = _ffn_body(x, g_ref, wup_ref, wconv_ref, wdown_ref, carry_ref, d_ff=d_ff, ck=ck)
    o_ref[0] = _rmsnorm(y, fg_ref[...])


def _ffn(x, g, w_up, w_conv, w_down, attn=None, w_o=None, final_g=None):
    bsz, s, d = x.shape
    d_ff = w_down.shape[0]
    tm = min(TM_FFN, s)
    tile = pl.BlockSpec((1, tm, d), lambda b, t: (b, t, 0))
    ffn_specs = [_const_spec((1, d)), _const_spec((d, 2 * d_ff)), _const_spec((3, 2 * d_ff)), _const_spec((d_ff, d))]
    ffn_args = [g.reshape(1, d), w_up.astype(_BF16), w_conv, w_down.astype(_BF16)]
    if attn is None:
        body, name = _ffn_kernel, "ffn"
        in_specs, args = [tile] + ffn_specs, [x] + ffn_args
    else:
        body, name = _proj_ffn_norm_kernel, "proj_ffn_norm"
        in_specs = [tile, tile, _const_spec((d, d))] + ffn_specs + [_const_spec((1, d))]
        args = [x, attn, w_o.astype(_BF16)] + ffn_args + [final_g.reshape(1, d)]
    return pl.pallas_call(
        functools.partial(body, d_ff=d_ff, ck=CK_FFN),
        grid=(bsz, s // tm),
        in_specs=in_specs,
        out_specs=tile,
        out_shape=jax.ShapeDtypeStruct(x.shape, _F32),
        scratch_shapes=[pltpu.VMEM((CARRY_ROWS, 2 * d_ff), _F32)],
        compiler_params=pltpu.CompilerParams(
            dimension_semantics=("arbitrary", "arbitrary"), vmem_limit_bytes=VMEM_LIMIT),
        name=name,
    )(*args)


def _rope_table_kernel(inv_row_ref, sign_row_ref, inv_col_ref, cos_k_ref, sin_k_ref, cos_t_ref, sin_t_ref):
    base = pl.program_id(0) * MOBA_BLOCK
    pos_rows = (base + lax.broadcasted_iota(jnp.int32, cos_k_ref.shape, 0)).astype(_F32)
    ang = pos_rows * inv_row_ref[...]
    cos_k_ref[...] = jnp.cos(ang)
    sin_k_ref[...] = jnp.sin(ang) * sign_row_ref[...]
    pos_cols = (base + lax.broadcasted_iota(jnp.int32, cos_t_ref.shape[1:], 1)).astype(_F32)
    ang_t = pos_cols * inv_col_ref[...]
    cos_t_ref[0] = jnp.cos(ang_t)
    sin_t_ref[0] = jnp.sin(ang_t)


def _rope_tables(s, head_dim):
    half = head_dim // 2
    nb = s // MOBA_BLOCK
    inv = ROPE_THETA ** (-jnp.arange(half, dtype=_F32) / half)
    inv_row = jnp.tile(inv, LANES // half).reshape(1, LANES)
    sign_row = jnp.tile(jnp.concatenate([-jnp.ones(half, _F32), jnp.ones(half, _F32)]),
                        LANES // head_dim).reshape(1, LANES)
    inv_col = jnp.broadcast_to(inv[:, None], (half, MOBA_BLOCK))
    return pl.pallas_call(
        _rope_table_kernel,
        grid=(nb,),
        in_specs=[pl.BlockSpec((1, LANES), lambda t: (0, 0)), pl.BlockSpec((1, LANES), lambda t: (0, 0)),
                  pl.BlockSpec((half, MOBA_BLOCK), lambda t: (0, 0))],
        out_specs=[pl.BlockSpec((MOBA_BLOCK, LANES), lambda t: (t, 0)),
                   pl.BlockSpec((MOBA_BLOCK, LANES), lambda t: (t, 0)),
                   pl.BlockSpec((1, half, MOBA_BLOCK), lambda t: (t, 0, 0)),
                   pl.BlockSpec((1, half, MOBA_BLOCK), lambda t: (t, 0, 0))],
        out_shape=[jax.ShapeDtypeStruct((s, LANES), _F32), jax.ShapeDtypeStruct((s, LANES), _F32),
                   jax.ShapeDtypeStruct((nb, half, MOBA_BLOCK), _F32),
                   jax.ShapeDtypeStruct((nb, half, MOBA_BLOCK), _F32)],
        name="rope_tables",
    )(inv_row, sign_row, inv_col)


def _qkv_kernel(x_ref, g_ref, wqt_ref, wk_ref, wvt_ref, cos_k_ref, sin_k_ref, cos_t_ref, sin_t_ref,
                qt_ref, k_ref, vt_ref, kmean_ref, *, head_dim):
    x = x_ref[0]
    d = x.shape[-1]
    half = head_dim // 2
    h = _rmsnorm(x, g_ref[...]).astype(_BF16)
    nt = (((1,), (1,)), ((), ()))

    qt = lax.dot_general(wqt_ref[...], h, nt, preferred_element_type=_F32)
    cos_t, sin_t = cos_t_ref[0], sin_t_ref[0]
    q_scale = head_dim ** -0.5 * LOG2E
    for lo in range(0, d, head_dim):
        x1, x2 = qt[lo:lo + half], qt[lo + half:lo + head_dim]
        qt_ref[0, 0, lo:lo + half, :] = ((x1 * cos_t - x2 * sin_t) * q_scale).astype(_BF16)
        qt_ref[0, 0, lo + half:lo + head_dim, :] = ((x2 * cos_t + x1 * sin_t) * q_scale).astype(_BF16)

    vt = lax.dot_general(wvt_ref[...], h, nt, preferred_element_type=_F32).astype(_BF16)
    v_rows = head_dim + ONES_ROWS
    for hd in range(d // head_dim):
        vt_ref[0, 0, hd * v_rows:hd * v_rows + head_dim, :] = vt[hd * head_dim:(hd + 1) * head_dim]
        vt_ref[0, 0, hd * v_rows + head_dim:(hd + 1) * v_rows, :] = jnp.ones((ONES_ROWS, MOBA_BLOCK), _BF16)

    cos_k, sin_k = cos_k_ref[...], sin_k_ref[...]
    lane = lax.broadcasted_iota(jnp.int32, cos_k.shape, 1)
    first_half = (lane % head_dim) < half
    for lo in range(0, d, LANES):
        kk = _dot(h, wk_ref[:, lo:lo + LANES])
        partner = jnp.where(first_half, pltpu.roll(kk, LANES - half, axis=1), pltpu.roll(kk, half, axis=1))
        roped = kk * cos_k + partner * sin_k
        k_ref[0, :, lo:lo + LANES] = roped.astype(_BF16)
        kmean_ref[0, 0, :, lo:lo + LANES] = jnp.mean(roped, axis=0, keepdims=True)


def _qkv(x, g, w_qkv, tables):
    bsz, s, d = x.shape
    head_dim = d // N_HEADS
    half = head_dim // 2
    nb = s // MOBA_BLOCK
    wq, wk, wv = jnp.split(w_qkv.astype(_BF16), 3, axis=1)
    cos_k, sin_k, cos_t, sin_t = tables
    tile = pl.BlockSpec((1, MOBA_BLOCK, d), lambda b, t: (b, t, 0))
    tposed = pl.BlockSpec((1, 1, d, MOBA_BLOCK), lambda b, t: (b, t, 0, 0))
    d_aug = N_HEADS * (head_dim + ONES_ROWS)
    v_tposed = pl.BlockSpec((1, 1, d_aug, MOBA_BLOCK), lambda b, t: (b, t, 0, 0))
    k_table = pl.BlockSpec((MOBA_BLOCK, LANES), lambda b, t: (t, 0))
    t_table = pl.BlockSpec((1, half, MOBA_BLOCK), lambda b, t: (t, 0, 0))
    return pl.pallas_call(
        functools.partial(_qkv_kernel, head_dim=head_dim),
        grid=(bsz, nb),
        in_specs=[tile, _const_spec((1, d)), _const_spec((d, d)), _const_spec((d, d)), _const_spec((d, d)),
                  k_table, k_table, t_table, t_table],
        out_specs=[tposed, tile, v_tposed, pl.BlockSpec((1, 1, 1, d), lambda b, t: (b, t, 0, 0))],
        out_shape=[jax.ShapeDtypeStruct((bsz, nb, d, MOBA_BLOCK), _BF16),
                   jax.ShapeDtypeStruct((bsz, s, d), _BF16),
                   jax.ShapeDtypeStruct((bsz, nb, d_aug, MOBA_BLOCK), _BF16),
                   jax.ShapeDtypeStruct((bsz, nb, 1, d), _F32)],
        compiler_params=pltpu.CompilerParams(
            dimension_semantics=("arbitrary", "arbitrary"), vmem_limit_bytes=VMEM_LIMIT),
        name="qkv_rope",
    )(x, g.reshape(1, d), wq.T, wk, wv.T, cos_k, sin_k, cos_t, sin_t)


def _moba_kernel(qt_ref, k_ref, vt_ref, kmean_ref, o_ref, q_ref, bias_ref, acc_ref, *, head_dim):
    own = pl.program_id(2)
    nb = kmean_ref.shape[1]
    blk = MOBA_BLOCK
    qt_pair = qt_ref[0, 0]
    kmean = kmean_ref[0].astype(_BF16)
    feat = lax.broadcasted_iota(jnp.int32, qt_pair.shape, 0)
    block_id = lax.broadcasted_iota(jnp.int32, (nb, blk), 0)
    key_pos = lax.broadcasted_iota(jnp.int32, (blk, blk), 0)
    query_pos = lax.broadcasted_iota(jnp.int32, (blk, blk), 1)
    k_own = k_ref[0, pl.ds(pl.multiple_of(own * blk, blk), blk), :]

    v_rows = head_dim + ONES_ROWS
    causal = key_pos <= query_pos

    def k_block(j):
        return k_ref[0, pl.ds(pl.multiple_of(j * blk, blk), blk), :]

    def v_block(j, hh):
        return vt_ref[0, j, hh * v_rows:(hh + 1) * v_rows, :]

    q_heads, stabilisers = [], []
    for hh in range(HEADS_PER_STEP):
        q_h = jnp.where((feat >= hh * head_dim) & (feat < (hh + 1) * head_dim), qt_pair, jnp.zeros_like(qt_pair))
        q_heads.append(q_h)
        cols = slice(hh * blk, (hh + 1) * blk)
        q_ref[:, cols] = q_h

        gate = jnp.where(block_id < own, _dot(kmean, q_h), NEG_INF)
        bias = jnp.full((nb, blk), NEG_INF, _F32)
        for _ in range(MOBA_TOPK):
            best = jnp.max(gate, axis=0, keepdims=True)
            first = jnp.min(jnp.where(gate == best, block_id, nb), axis=0, keepdims=True)
            picked = block_id == first
            bias = jnp.where(picked, 0.0, bias)
            gate = jnp.where(picked, BELOW_NEG_INF, gate)

        s = jnp.where(causal, _dot(k_own, q_h), NEG_INF)
        m = jnp.max(s, axis=0, keepdims=True)
        stabilisers.append(m)
        bias_ref[:, cols] = jnp.where(block_id < own, bias, NEG_INF) - m
        acc_ref[hh] = _dot(v_block(own, hh), jnp.exp2(s - m).astype(_BF16))

    def visit_group(i, peak):
        j0 = i * BLOCKS_PER_VISIT
        keys = k_ref[0, pl.ds(pl.multiple_of(j0 * blk, blk), BLOCKS_PER_VISIT * blk), :]
        s = _dot(keys, q_ref[...])
        updates = [None] * HEADS_PER_STEP
        for g in range(BLOCKS_PER_VISIT):
            t = s[g * blk:(g + 1) * blk] + bias_ref[pl.ds(j0 + g, 1), :]
            peak = jnp.maximum(peak, jnp.max(t.reshape(blk // 8, 8, HEADS_PER_STEP * blk), axis=0))
            p = jnp.exp2(t).astype(_BF16)
            for hh in range(HEADS_PER_STEP):
                u = _dot(v_block(j0 + g, hh), p[:, hh * blk:(hh + 1) * blk])
                updates[hh] = u if updates[hh] is None else updates[hh] + u
        for hh in range(HEADS_PER_STEP):
            acc_ref[hh] += updates[hh]
        return peak

    n_groups = (own + BLOCKS_PER_VISIT - 1) // BLOCKS_PER_VISIT
    peak = lax.fori_loop(0, n_groups, visit_group, jnp.full((8, HEADS_PER_STEP * blk), NEG_INF, _F32))
    overflow_risk = jnp.max(peak) > OVERFLOW_GUARD

    @pl.when(overflow_risk)
    def _():
        for hh in range(HEADS_PER_STEP):
            s = jnp.where(causal, _dot(k_own, q_heads[hh]), NEG_INF) - stabilisers[hh]
            acc = _dot(v_block(own, hh), jnp.exp2(s).astype(_BF16))

            def visit_exact(j, carry, hh=hh):
                m_run, acc = carry
                t = _dot(k_block(j), q_heads[hh]) + bias_ref[pl.ds(j, 1), hh * blk:(hh + 1) * blk]
                m_new = jnp.maximum(m_run, jnp.max(t, axis=0, keepdims=True))
                acc = jnp.exp2(m_run - m_new) * acc + _dot(v_block(j, hh), jnp.exp2(t - m_new).astype(_BF16))
                return m_new, acc

            _, acc = lax.fori_loop(0, own, visit_exact, (jnp.zeros((1, blk), _F32), acc))
            acc_ref[hh] = acc

    o_t = jnp.concatenate([acc_ref[hh, :head_dim] / acc_ref[hh, head_dim:head_dim + 1]
                           for hh in range(HEADS_PER_STEP)], axis=0)
    o_ref[0] = o_t.T.astype(_BF16)


def _moba(qt, k, vt, kmean):
    bsz, s, d = k.shape
    head_dim = d // N_HEADS
    nb = s // MOBA_BLOCK
    width = HEADS_PER_STEP * head_dim
    v_rows = head_dim + ONES_ROWS
    return pl.pallas_call(
        functools.partial(_moba_kernel, head_dim=head_dim),
        grid=(bsz, d // width, nb),
        in_specs=[pl.BlockSpec((1, 1, width, MOBA_BLOCK), lambda b, c, i: (b, i, c, 0)),
                  pl.BlockSpec((1, s, width), lambda b, c, i: (b, 0, c)),
                  pl.BlockSpec((1, nb, HEADS_PER_STEP * v_rows, MOBA_BLOCK), lambda b, c, i: (b, 0, c, 0)),
                  pl.BlockSpec((1, nb, width), lambda b, c, i: (b, 0, c))],
        out_specs=pl.BlockSpec((1, MOBA_BLOCK, width), lambda b, c, i: (b, i, c)),
        out_shape=jax.ShapeDtypeStruct((bsz, s, d), _BF16),
        scratch_shapes=[pltpu.VMEM((width, HEADS_PER_STEP * MOBA_BLOCK), _BF16),
                        pltpu.VMEM((nb, HEADS_PER_STEP * MOBA_BLOCK), _F32),
                        pltpu.VMEM((HEADS_PER_STEP, v_rows, MOBA_BLOCK), _F32)],
        compiler_params=pltpu.CompilerParams(
            dimension_semantics=("arbitrary", "arbitrary", "arbitrary"), vmem_limit_bytes=VMEM_LIMIT),
        name="moba_attention",
    )(qt, k, vt, kmean.reshape(bsz, nb, d))


def kernel(x, mix_norm, sc_w_in, sc_w_conv, sc_w_out, moba_w_qkv, moba_w_o, ffn_norm, ffn_w_up, ffn_w_conv,
           ffn_w_down, final_norm):
    bsz, s, d = x.shape
    assert N_HEADS % HEADS_PER_STEP == 0 and (d // N_HEADS * HEADS_PER_STEP) % LANES == 0
    assert s % (MOBA_BLOCK * BLOCKS_PER_VISIT) == 0 and s % min(TM_FFN, s) == 0 and s % min(TM_MIXER, s) == 0
    assert mix_norm.shape[0] == 2 and ffn_w_down.shape[1] % CK_FFN == 0 and d % CK_MIXER == 0

    x = _mixer(x, mix_norm[0], sc_w_in[0], sc_w_conv[0], sc_w_out[0])
    x = _ffn(x, ffn_norm[0], ffn_w_up[0], ffn_w_conv[0], ffn_w_down[0])
    qt, k, vt, kmean = _qkv(x, mix_norm[1], moba_w_qkv[0], _rope_tables(s, d // N_HEADS))
    attn = _moba(qt, k, vt, kmean)
    return _ffn(x, ffn_norm[1], ffn_w_up[1], ffn_w_conv[1], ffn_w_down[1],
                attn=attn, w_o=moba_w_o[0], final_g=final_norm)
```

```python
import functools

import jax
import jax.numpy as jnp
from jax import lax
from jax.experimental import pallas as pl
from jax.experimental.pallas import tpu as pltpu

N_HEADS = 16
MOBA_BLOCK = 256
MOBA_TOPK = 3
ROPE_THETA = 10000.0
RMS_EPS = 1e-6
NEG_INF = -1e30
BELOW_NEG_INF = -3.0e38
LOG2E = 1.4426950408889634
CARRY_ROWS = 8
LANES = 128
HEADS_PER_STEP = 4
ONES_ROWS = 16
BLOCKS_PER_VISIT = 8
EXPONENT_GUARD = 64.0

TM_MIXER = 512
CK_MIXER = 512
TM_FFN = 256
CK_FFN = 256
VMEM_LIMIT = 56 * 1024 * 1024

_BF16 = jnp.bfloat16
_F32 = jnp.float32


def _dot(a, b):
    return jnp.dot(a, b, preferred_element_type=_F32)


def _rmsnorm(x, g):
    var = jnp.mean(x * x, axis=-1, keepdims=True)
    return x * lax.rsqrt(var + RMS_EPS) * g


def _delay_rows(u, prev, n):
    rolled = pltpu.roll(u, n, axis=0)
    tail = pltpu.roll(prev, n, axis=0)
    row = lax.broadcasted_iota(jnp.int32, prev.shape, 0)
    first = jnp.where(row < n, tail, rolled[:CARRY_ROWS])
    return jnp.concatenate([first, rolled[CARRY_ROWS:]], axis=0)


def _causal_conv3(u, prev, w):
    return _delay_rows(u, prev, 2) * w[0:1] + _delay_rows(u, prev, 1) * w[1:2] + u * w[2:3]


def _reset_carry_at_sequence_start(carry_ref):
    @pl.when(pl.program_id(1) == 0)
    def _():
        carry_ref[...] = jnp.zeros_like(carry_ref)


def _mixer_kernel(x_ref, g_ref, win_ref, wconv_ref, wout_ref, o_ref, carry_ref, *, ck):
    _reset_carry_at_sequence_start(carry_ref)
    x = x_ref[0]
    d = x.shape[-1]
    h = _rmsnorm(x, g_ref[...]).astype(_BF16)
    acc = x
    for lo in range(0, d, ck):
        bb = _dot(h, win_ref[:, lo:lo + ck])
        cv = _dot(h, win_ref[:, d + lo:d + lo + ck]) * _dot(h, win_ref[:, 2 * d + lo:2 * d + lo + ck])
        y = _causal_conv3(cv, carry_ref[:, lo:lo + ck], wconv_ref[:, lo:lo + ck])
        carry_ref[:, lo:lo + ck] = cv[-CARRY_ROWS:, :]
        acc = acc + _dot((bb * y).astype(_BF16), wout_ref[lo:lo + ck, :])
    o_ref[0] = acc


def _const_spec(shape):
    zeros = (0,) * len(shape)
    return pl.BlockSpec(shape, lambda *_: zeros, pipeline_mode=pl.Buffered(1))


def _mixer(x, g, w_in, w_conv, w_out):
    bsz, s, d = x.shape
    tm = min(TM_MIXER, s)
    tile = pl.BlockSpec((1, tm, d), lambda b, t: (b, t, 0))
    return pl.pallas_call(
        functools.partial(_mixer_kernel, ck=CK_MIXER),
        grid=(bsz, s // tm),
        in_specs=[tile, _const_spec((1, d)), _const_spec((d, 3 * d)), _const_spec((3, d)), _const_spec((d, d))],
        out_specs=tile,
        out_shape=jax.ShapeDtypeStruct(x.shape, _F32),
        scratch_shapes=[pltpu.VMEM((CARRY_ROWS, d), _F32)],
        compiler_params=pltpu.CompilerParams(
            dimension_semantics=("arbitrary", "arbitrary"), vmem_limit_bytes=VMEM_LIMIT),
        name="mixer",
    )(x, g.reshape(1, d), w_in.astype(_BF16), w_conv, w_out.astype(_BF16))


def _ffn_body(x, g_ref, wup_ref, wconv_ref, wdown_ref, carry_ref, *, d_ff, ck):
    h = _rmsnorm(x, g_ref[...]).astype(_BF16)

    def up_project(lo):
        halves = []
        for col in (lo, d_ff + lo):
            up = _dot(h, wup_ref[:, col:col + ck])
            halves.append((up, carry_ref[:, col:col + ck]))
            carry_ref[:, col:col + ck] = up[-CARRY_ROWS:, :]
        return halves

    chunks = list(range(0, d_ff, ck))
    acc = x
    ahead = up_project(chunks[0])
    for c, lo in enumerate(chunks):
        gate, lin = (_causal_conv3(up, prev, wconv_ref[:, col:col + ck])
                     for (up, prev), col in zip(ahead, (lo, d_ff + lo)))
        act = (gate / (1.0 + jnp.exp(-gate)) * lin).astype(_BF16)
        if c + 1 < len(chunks):
            ahead = up_project(chunks[c + 1])
        acc = acc + _dot(act, wdown_ref[lo:lo + ck, :])
    return acc


def _ffn_kernel(x_ref, g_ref, wup_ref, wconv_ref, wdown_ref, o_ref, carry_ref, *, d_ff, ck):
    _reset_carry_at_sequence_start(carry_ref)
    o_ref[0] = _ffn_body(x_ref[0], g_ref, wup_ref, wconv_ref, wdown_ref, carry_ref, d_ff=d_ff, ck=ck)


def _proj_ffn_norm_kernel(x_ref, a_ref, wo_ref, g_ref, wup_ref, wconv_ref, wdown_ref, fg_ref, o_ref, carry_ref,
                          *, d_ff, ck):
    _reset_carry_at_sequence_start(carry_ref)
    x = x_ref[0] + _dot(a_ref[0], wo_ref[...])
    y = _ffn_body(x, g_ref, wup_ref, wconv_ref, wdown_ref, carry_ref, d_ff=d_ff, ck=ck)
    o_ref[0] = _rmsnorm(y, fg_ref[...])


def _ffn(x, g, w_up, w_conv, w_down, attn=None, w_o=None, final_g=None):
    bsz, s, d = x.shape
    d_ff = w_down.shape[0]
    tm = min(TM_FFN, s)
    tile = pl.BlockSpec((1, tm, d), lambda b, t: (b, t, 0))
    ffn_specs = [_const_spec((1, d)), _const_spec((d, 2 * d_ff)), _const_spec((3, 2 * d_ff)), _const_spec((d_ff, d))]
    ffn_args = [g.reshape(1, d), w_up.astype(_BF16), w_conv, w_down.astype(_BF16)]
    if attn is None:
        body, name = _ffn_kernel, "ffn"
        in_specs, args = [tile] + ffn_specs, [x] + ffn_args
    else:
        body, name = _proj_ffn_norm_kernel, "proj_ffn_norm"
        in_specs = [tile, tile, _const_spec((d, d))] + ffn_specs + [_const_spec((1, d))]
        args = [x, attn, w_o.astype(_BF16)] + ffn_args + [final_g.reshape(1, d)]
    return pl.pallas_call(
        functools.partial(body, d_ff=d_ff, ck=CK_FFN),
        grid=(bsz, s // tm),
        in_specs=in_specs,
        out_specs=tile,
        out_shape=jax.ShapeDtypeStruct(x.shape, _F32),
        scratch_shapes=[pltpu.VMEM((CARRY_ROWS, 2 * d_ff), _F32)],
        compiler_params=pltpu.CompilerParams(
            dimension_semantics=("arbitrary", "arbitrary"), vmem_limit_bytes=VMEM_LIMIT),
        name=name,
    )(*args)


def _rope_table_kernel(inv_row_ref, sign_row_ref, inv_col_ref, cos_k_ref, sin_k_ref, cos_t_ref, sin_t_ref):
    base = pl.program_id(0) * MOBA_BLOCK
    pos_rows = (base + lax.broadcasted_iota(jnp.int32, cos_k_ref.shape, 0)).astype(_F32)
    ang = pos_rows * inv_row_ref[...]
    cos_k_ref[...] = jnp.cos(ang)
    sin_k_ref[...] = jnp.sin(ang) * sign_row_ref[...]
    pos_cols = (base + lax.broadcasted_iota(jnp.int32, cos_t_ref.shape[1:], 1)).astype(_F32)
    ang_t = pos_cols * inv_col_ref[...]
    cos_t_ref[0] = jnp.cos(ang_t)
    sin_t_ref[0] = jnp.sin(ang_t)


def _rope_tables(s, head_dim):
    half = head_dim // 2
    nb = s // MOBA_BLOCK
    inv = ROPE_THETA ** (-jnp.arange(half, dtype=_F32) / half)
    inv_row = jnp.tile(inv, LANES // half).reshape(1, LANES)
    sign_row = jnp.tile(jnp.concatenate([-jnp.ones(half, _F32), jnp.ones(half, _F32)]),
                        LANES // head_dim).reshape(1, LANES)
    inv_col = jnp.broadcast_to(inv[:, None], (half, MOBA_BLOCK))
    return pl.pallas_call(
        _rope_table_kernel,
        grid=(nb,),
        in_specs=[pl.BlockSpec((1, LANES), lambda t: (0, 0)), pl.BlockSpec((1, LANES), lambda t: (0, 0)),
                  pl.BlockSpec((half, MOBA_BLOCK), lambda t: (0, 0))],
        out_specs=[pl.BlockSpec((MOBA_BLOCK, LANES), lambda t: (t, 0)),
                   pl.BlockSpec((MOBA_BLOCK, LANES), lambda t: (t, 0)),
                   pl.BlockSpec((1, half, MOBA_BLOCK), lambda t: (t, 0, 0)),
                   pl.BlockSpec((1, half, MOBA_BLOCK), lambda t: (t, 0, 0))],
        out_shape=[jax.ShapeDtypeStruct((s, LANES), _F32), jax.ShapeDtypeStruct((s, LANES), _F32),
                   jax.ShapeDtypeStruct((nb, half, MOBA_BLOCK), _F32),
                   jax.ShapeDtypeStruct((nb, half, MOBA_BLOCK), _F32)],
        name="rope_tables",
    )(inv_row, sign_row, inv_col)


def _qkv_kernel(x_ref, g_ref, wqt_ref, wk_ref, wvt_ref, cos_k_ref, sin_k_ref, cos_t_ref, sin_t_ref,
                qt_ref, k_ref, vt_ref, kmean_ref, *, head_dim):
    x = x_ref[0]
    d = x.shape[-1]
    half = head_dim // 2
    h = _rmsnorm(x, g_ref[...]).astype(_BF16)
    nt = (((1,), (1,)), ((), ()))

    qt = lax.dot_general(wqt_ref[...], h, nt, preferred_element_type=_F32)
    cos_t, sin_t = cos_t_ref[0], sin_t_ref[0]
    q_scale = head_dim ** -0.5 * LOG2E
    for lo in range(0, d, head_dim):
        x1, x2 = qt[lo:lo + half], qt[lo + half:lo + head_dim]
        qt_ref[0, 0, lo:lo + half, :] = ((x1 * cos_t - x2 * sin_t) * q_scale).astype(_BF16)
        qt_ref[0, 0, lo + half:lo + head_dim, :] = ((x2 * cos_t + x1 * sin_t) * q_scale).astype(_BF16)

    vt = lax.dot_general(wvt_ref[...], h, nt, preferred_element_type=_F32).astype(_BF16)
    v_rows = head_dim + ONES_ROWS
    for hd in range(d // head_dim):
        vt_ref[0, 0, hd * v_rows:hd * v_rows + head_dim, :] = vt[hd * head_dim:(hd + 1) * head_dim]
        vt_ref[0, 0, hd * v_rows + head_dim:(hd + 1) * v_rows, :] = jnp.ones((ONES_ROWS, MOBA_BLOCK), _BF16)

    cos_k, sin_k = cos_k_ref[...], sin_k_ref[...]
    lane = lax.broadcasted_iota(jnp.int32, cos_k.shape, 1)
    first_half = (lane % head_dim) < half
    for lo in range(0, d, LANES):
        kk = _dot(h, wk_ref[:, lo:lo + LANES])
        partner = jnp.where(first_half, pltpu.roll(kk, LANES - half, axis=1), pltpu.roll(kk, half, axis=1))
        roped = kk * cos_k + partner * sin_k
        k_ref[0, :, lo:lo + LANES] = roped.astype(_BF16)
        kmean_ref[0, 0, :, lo:lo + LANES] = jnp.mean(roped, axis=0, keepdims=True)


def _qkv(x, g, w_qkv, tables):
    bsz, s, d = x.shape
    head_dim = d // N_HEADS
    half = head_dim // 2
    nb = s // MOBA_BLOCK
    wq, wk, wv = jnp.split(w_qkv.astype(_BF16), 3, axis=1)
    cos_k, sin_k, cos_t, sin_t = tables
    tile = pl.BlockSpec((1, MOBA_BLOCK, d), lambda b, t: (b, t, 0))
    tposed = pl.BlockSpec((1, 1, d, MOBA_BLOCK), lambda b, t: (b, t, 0, 0))
    d_aug = N_HEADS * (head_dim + ONES_ROWS)
    v_tposed = pl.BlockSpec((1, 1, d_aug, MOBA_BLOCK), lambda b, t: (b, t, 0, 0))
    k_table = pl.BlockSpec((MOBA_BLOCK, LANES), lambda b, t: (t, 0))
    t_table = pl.BlockSpec((1, half, MOBA_BLOCK), lambda b, t: (t, 0, 0))
    return pl.pallas_call(
        functools.partial(_qkv_kernel, head_dim=head_dim),
        grid=(bsz, nb),
        in_specs=[tile, _const_spec((1, d)), _const_spec((d, d)), _const_spec((d, d)), _const_spec((d, d)),
                  k_table, k_table, t_table, t_table],
        out_specs=[tposed, tile, v_tposed, pl.BlockSpec((1, 1, 1, d), lambda b, t: (b, t, 0, 0))],
        out_shape=[jax.ShapeDtypeStruct((bsz, nb, d, MOBA_BLOCK), _BF16),
                   jax.ShapeDtypeStruct((bsz, s, d), _BF16),
                   jax.ShapeDtypeStruct((bsz, nb, d_aug, MOBA_BLOCK), _BF16),
                   jax.ShapeDtypeStruct((bsz, nb, 1, d), _F32)],
        compiler_params=pltpu.CompilerParams(
            dimension_semantics=("arbitrary", "arbitrary"), vmem_limit_bytes=VMEM_LIMIT),
        name="qkv_rope",
    )(x, g.reshape(1, d), wq.T, wk, wv.T, cos_k, sin_k, cos_t, sin_t)


def _moba_kernel(qt_ref, k_ref, vt_ref, kmean_ref, o_ref, q_ref, bias_ref, acc_ref, *, head_dim):
    own = pl.program_id(2)
    nb = kmean_ref.shape[1]
    blk = MOBA_BLOCK
    qt_pair = qt_ref[0, 0]
    kmean = kmean_ref[0].astype(_BF16)
    feat = lax.broadcasted_iota(jnp.int32, qt_pair.shape, 0)
    block_id = lax.broadcasted_iota(jnp.int32, (nb, blk), 0)
    key_pos = lax.broadcasted_iota(jnp.int32, (blk, blk), 0)
    query_pos = lax.broadcasted_iota(jnp.int32, (blk, blk), 1)
    k_own = k_ref[0, pl.ds(pl.multiple_of(own * blk, blk), blk), :]

    v_rows = head_dim + ONES_ROWS
    causal = key_pos <= query_pos

    def k_block(j):
        return k_ref[0, pl.ds(pl.multiple_of(j * blk, blk), blk), :]

    def v_block(j, hh):
        return vt_ref[0, j, hh * v_rows:(hh + 1) * v_rows, :]

    q_heads, own_peaks = [], []
    for hh in range(HEADS_PER_STEP):
        q_h = jnp.where((feat >= hh * head_dim) & (feat < (hh + 1) * head_dim), qt_pair, jnp.zeros_like(qt_pair))
        q_heads.append(q_h)
        cols = slice(hh * blk, (hh + 1) * blk)
        q_ref[:, cols] = q_h

        gate = jnp.where(block_id < own, _dot(kmean, q_h), NEG_INF)
        bias = jnp.full((nb, blk), NEG_INF, _F32)
        for _ in range(MOBA_TOPK):
            best = jnp.max(gate, axis=0, keepdims=True)
            first = jnp.min(jnp.where(gate == best, block_id, nb), axis=0, keepdims=True)
            picked = block_id == first
            bias = jnp.where(picked, 0.0, bias)
            gate = jnp.where(picked, BELOW_NEG_INF, gate)

        bias_ref[:, cols] = jnp.where(block_id < own, bias, NEG_INF)

        s = jnp.where(causal, _dot(k_own, q_h), NEG_INF)
        own_peaks.append(jnp.max(s.reshape(blk // 8, 8, blk), axis=0))
        acc_ref[hh] = _dot(v_block(own, hh), jnp.exp2(s).astype(_BF16))

    def visit_group(i, peak):
        j0 = i * BLOCKS_PER_VISIT
        keys = k_ref[0, pl.ds(pl.multiple_of(j0 * blk, blk), BLOCKS_PER_VISIT * blk), :]
        s = _dot(keys, q_ref[...])
        updates = [None] * HEADS_PER_STEP
        for g in range(BLOCKS_PER_VISIT):
            t = s[g * blk:(g + 1) * blk] + bias_ref[pl.ds(j0 + g, 1), :]
            peak = jnp.maximum(peak, jnp.max(t.reshape(blk // 8, 8, HEADS_PER_STEP * blk), axis=0))
            p = jnp.exp2(t).astype(_BF16)
            for hh in range(HEADS_PER_STEP):
                u = _dot(v_block(j0 + g, hh), p[:, hh * blk:(hh + 1) * blk])
                updates[hh] = u if updates[hh] is None else updates[hh] + u
        for hh in range(HEADS_PER_STEP):
            acc_ref[hh] += updates[hh]
        return peak

    n_groups = (own + BLOCKS_PER_VISIT - 1) // BLOCKS_PER_VISIT
    peak = lax.fori_loop(0, n_groups, visit_group, jnp.concatenate(own_peaks, axis=1))
    query_peak = jnp.max(peak, axis=0, keepdims=True)
    out_of_range = (jnp.max(query_peak) > EXPONENT_GUARD) | (jnp.min(query_peak) < -EXPONENT_GUARD)

    @pl.when(out_of_range)
    def _():
        for hh in range(HEADS_PER_STEP):
            s = jnp.where(causal, _dot(k_own, q_heads[hh]), NEG_INF)
            m_own = jnp.max(s, axis=0, keepdims=True)
            acc = _dot(v_block(own, hh), jnp.exp2(s - m_own).astype(_BF16))

            def visit_exact(j, carry, hh=hh):
                m_run, acc = carry
                t = _dot(k_block(j), q_heads[hh]) + bias_ref[pl.ds(j, 1), hh * blk:(hh + 1) * blk]
                m_new = jnp.maximum(m_run, jnp.max(t, axis=0, keepdims=True))
                acc = jnp.exp2(m_run - m_new) * acc + _dot(v_block(j, hh), jnp.exp2(t - m_new).astype(_BF16))
                return m_new, acc

            _, acc = lax.fori_loop(0, own, visit_exact, (m_own, acc))
            acc_ref[hh] = acc

    o_t = jnp.concatenate([acc_ref[hh, :head_dim] / acc_ref[hh, head_dim:head_dim + 1]
                           for hh in range(HEADS_PER_STEP)], axis=0)
    o_ref[0] = o_t.T.astype(_BF16)


def _moba(qt, k, vt, kmean):
    bsz, s, d = k.shape
    head_dim = d // N_HEADS
    nb = s // MOBA_BLOCK
    width = HEADS_PER_STEP * head_dim
    v_rows = head_dim + ONES_ROWS
    return pl.pallas_call(
        functools.partial(_moba_kernel, head_dim=head_dim),
        grid=(bsz, d // width, nb),
        in_specs=[pl.BlockSpec((1, 1, width, MOBA_BLOCK), lambda b, c, i: (b, i, c, 0)),
                  pl.BlockSpec((1, s, width), lambda b, c, i: (b, 0, c)),
                  pl.BlockSpec((1, nb, HEADS_PER_STEP * v_rows, MOBA_BLOCK), lambda b, c, i: (b, 0, c, 0)),
                  pl.BlockSpec((1, nb, width), lambda b, c, i: (b, 0, c))],
        out_specs=pl.BlockSpec((1, MOBA_BLOCK, width), lambda b, c, i: (b, i, c)),
        out_shape=jax.ShapeDtypeStruct((bsz, s, d), _BF16),
        scratch_shapes=[pltpu.VMEM((width, HEADS_PER_STEP * MOBA_BLOCK), _BF16),
                        pltpu.VMEM((nb, HEADS_PER_STEP * MOBA_BLOCK), _F32),
                        pltpu.VMEM((HEADS_PER_STEP, v_rows, MOBA_BLOCK), _F32)],
        compiler_params=pltpu.CompilerParams(
            dimension_semantics=("arbitrary", "arbitrary", "arbitrary"), vmem_limit_bytes=VMEM_LIMIT),
        name="moba_attention",
    )(qt, k, vt, kmean.reshape(bsz, nb, d))


def kernel(x, mix_norm, sc_w_in, sc_w_conv, sc_w_out, moba_w_qkv, moba_w_o, ffn_norm, ffn_w_up, ffn_w_conv,
           ffn_w_down, final_norm):
    bsz, s, d = x.shape
    assert N_HEADS % HEADS_PER_STEP == 0 and (d // N_HEADS * HEADS_PER_STEP) % LANES == 0
    assert s % (MOBA_BLOCK * BLOCKS_PER_VISIT) == 0 and s % min(TM_FFN, s) == 0 and s % min(TM_MIXER, s) == 0
    assert mix_norm.shape[0] == 2 and ffn_w_down.shape[1] % CK_FFN == 0 and d % CK_MIXER == 0

    x = _mixer(x, mix_norm[0], sc_w_in[0], sc_w_conv[0], sc_w_out[0])
    x = _ffn(x, ffn_norm[0], ffn_w_up[0], ffn_w_conv[0], ffn_w_down[0])
    qt, k, vt, kmean = _qkv(x, mix_norm[1], moba_w_qkv[0], _rope_tables(s, d // N_HEADS))
    attn = _moba(qt, k, vt, kmean)
    return _ffn(x, ffn_norm[1], ffn_w_up[1], ffn_w_conv[1], ffn_w_down[1],
                attn=attn, w_o=moba_w_o[0], final_g=final_norm)
```

```python
import functools

import jax
import jax.numpy as jnp
from jax import lax
from jax.experimental import pallas as pl
from jax.experimental.pallas import tpu as pltpu

N_HEADS = 16
MOBA_BLOCK = 256
MOBA_TOPK = 3
ROPE_THETA = 10000.0
RMS_EPS = 1e-6
NEG_INF = -1e30
BELOW_NEG_INF = -3.0e38
LOG2E = 1.4426950408889634
CARRY_ROWS = 8
LANES = 128
HEADS_PER_STEP = 4
ONES_ROWS = 16
BLOCKS_PER_VISIT = 8
TAIL_BLOCKS_PER_VISIT = 4
EXPONENT_GUARD = 64.0

TM_MIXER = 512
CK_MIXER = 512
TM_FFN = 256
CK_FFN = 256
VMEM_LIMIT = 56 * 1024 * 1024

_BF16 = jnp.bfloat16
_F32 = jnp.float32


def _dot(a, b):
    return jnp.dot(a, b, preferred_element_type=_F32)


def _rmsnorm(x, g):
    var = jnp.mean(x * x, axis=-1, keepdims=True)
    return x * lax.rsqrt(var + RMS_EPS) * g


def _delay_rows(u, prev, n):
    rolled = pltpu.roll(u, n, axis=0)
    tail = pltpu.roll(prev, n, axis=0)
    row = lax.broadcasted_iota(jnp.int32, prev.shape, 0)
    first = jnp.where(row < n, tail, rolled[:CARRY_ROWS])
    return jnp.concatenate([first, rolled[CARRY_ROWS:]], axis=0)


def _causal_conv3(u, prev, w):
    return _delay_rows(u, prev, 2) * w[0:1] + _delay_rows(u, prev, 1) * w[1:2] + u * w[2:3]


def _reset_carry_at_sequence_start(carry_ref):
    @pl.when(pl.program_id(1) == 0)
    def _():
        carry_ref[...] = jnp.zeros_like(carry_ref)


def _mixer_kernel(x_ref, g_ref, win_ref, wconv_ref, wout_ref, o_ref, carry_ref, *, ck):
    _reset_carry_at_sequence_start(carry_ref)
    x = x_ref[0]
    d = x.shape[-1]
    h = _rmsnorm(x, g_ref[...]).astype(_BF16)
    acc = x
    for lo in range(0, d, ck):
        bb = _dot(h, win_ref[:, lo:lo + ck])
        cv = _dot(h, win_ref[:, d + lo:d + lo + ck]) * _dot(h, win_ref[:, 2 * d + lo:2 * d + lo + ck])
        y = _causal_conv3(cv, carry_ref[:, lo:lo + ck], wconv_ref[:, lo:lo + ck])
        carry_ref[:, lo:lo + ck] = cv[-CARRY_ROWS:, :]
        acc = acc + _dot((bb * y).astype(_BF16), wout_ref[lo:lo + ck, :])
    o_ref[0] = acc


def _const_spec(shape):
    zeros = (0,) * len(shape)
    return pl.BlockSpec(shape, lambda *_: zeros, pipeline_mode=pl.Buffered(1))


def _mixer(x, g, w_in, w_conv, w_out):
    bsz, s, d = x.shape
    tm = min(TM_MIXER, s)
    tile = pl.BlockSpec((1, tm, d), lambda b, t: (b, t, 0))
    return pl.pallas_call(
        functools.partial(_mixer_kernel, ck=CK_MIXER),
        grid=(bsz, s // tm),
        in_specs=[tile, _const_spec((1, d)), _const_spec((d, 3 * d)), _const_spec((3, d)), _const_spec((d, d))],
        out_specs=tile,
        out_shape=jax.ShapeDtypeStruct(x.shape, _F32),
        scratch_shapes=[pltpu.VMEM((CARRY_ROWS, d), _F32)],
        compiler_params=pltpu.CompilerParams(
            dimension_semantics=("arbitrary", "arbitrary"), vmem_limit_bytes=VMEM_LIMIT),
        name="mixer",
    )(x, g.reshape(1, d), w_in.astype(_BF16), w_conv, w_out.astype(_BF16))


def _ffn_body(x, g_ref, wup_ref, wconv_ref, wdown_ref, carry_ref, *, d_ff, ck):
    h = _rmsnorm(x, g_ref[...]).astype(_BF16)

    def up_project(lo):
        halves = []
        for col in (lo, d_ff + lo):
            up = _dot(h, wup_ref[:, col:col + ck])
            halves.append((up, carry_ref[:, col:col + ck]))
            carry_ref[:, col:col + ck] = up[-CARRY_ROWS:, :]
        return halves

    chunks = list(range(0, d_ff, ck))
    acc = x
    ahead = up_project(chunks[0])
    for c, lo in enumerate(chunks):
        gate, lin = (_causal_conv3(up, prev, wconv_ref[:, col:col + ck])
                     for (up, prev), col in zip(ahead, (lo, d_ff + lo)))
        act = (gate / (1.0 + jnp.exp(-gate)) * lin).astype(_BF16)
        if c + 1 < len(chunks):
            ahead = up_project(chunks[c + 1])
        acc = acc + _dot(act, wdown_ref[lo:lo + ck, :])
    return acc


def _ffn_kernel(x_ref, g_ref, wup_ref, wconv_ref, wdown_ref, o_ref, carry_ref, *, d_ff, ck):
    _reset_carry_at_sequence_start(carry_ref)
    o_ref[0] = _ffn_body(x_ref[0], g_ref, wup_ref, wconv_ref, wdown_ref, carry_ref, d_ff=d_ff, ck=ck)


def _proj_ffn_norm_kernel(x_ref, a_ref, wo_ref, g_ref, wup_ref, wconv_ref, wdown_ref, fg_ref, o_ref, carry_ref,
                          *, d_ff, ck):
    _reset_carry_at_sequence_start(carry_ref)
    x = x_ref[0] + _dot(a_ref[0], wo_ref[...])
    y = _ffn_body(x, g_ref, wup_ref, wconv_ref, wdown_ref, carry_ref, d_ff=d_ff, ck=ck)
    o_ref[0] = _rmsnorm(y, fg_ref[...])


def _ffn(x, g, w_up, w_conv, w_down, attn=None, w_o=None, final_g=None):
    bsz, s, d = x.shape
    d_ff = w_down.shape[0]
    tm = min(TM_FFN, s)
    tile = pl.BlockSpec((1, tm, d), lambda b, t: (b, t, 0))
    ffn_specs = [_const_spec((1, d)), _const_spec((d, 2 * d_ff)), _const_spec((3, 2 * d_ff)), _const_spec((d_ff, d))]
    ffn_args = [g.reshape(1, d), w_up.astype(_BF16), w_conv, w_down.astype(_BF16)]
    if attn is None:
        body, name = _ffn_kernel, "ffn"
        in_specs, args = [tile] + ffn_specs, [x] + ffn_args
    else:
        body, name = _proj_ffn_norm_kernel, "proj_ffn_norm"
        in_specs = [tile, tile, _const_spec((d, d))] + ffn_specs + [_const_spec((1, d))]
        args = [x, attn, w_o.astype(_BF16)] + ffn_args + [final_g.reshape(1, d)]
    return pl.pallas_call(
        functools.partial(body, d_ff=d_ff, ck=CK_FFN),
        grid=(bsz, s // tm),
        in_specs=in_specs,
        out_specs=tile,
        out_shape=jax.ShapeDtypeStruct(x.shape, _F32),
        scratch_shapes=[pltpu.VMEM((CARRY_ROWS, 2 * d_ff), _F32)],
        compiler_params=pltpu.CompilerParams(
            dimension_semantics=("arbitrary", "arbitrary"), vmem_limit_bytes=VMEM_LIMIT),
        name=name,
    )(*args)


def _rope_table_kernel(inv_row_ref, sign_row_ref, inv_col_ref, cos_k_ref, sin_k_ref, cos_t_ref, sin_t_ref):
    base = pl.program_id(0) * MOBA_BLOCK
    pos_rows = (base + lax.broadcasted_iota(jnp.int32, cos_k_ref.shape, 0)).astype(_F32)
    ang = pos_rows * inv_row_ref[...]
    cos_k_ref[...] = jnp.cos(ang)
    sin_k_ref[...] = jnp.sin(ang) * sign_row_ref[...]
    pos_cols = (base + lax.broadcasted_iota(jnp.int32, cos_t_ref.shape[1:], 1)).astype(_F32)
    ang_t = pos_cols * inv_col_ref[...]
    cos_t_ref[0] = jnp.cos(ang_t)
    sin_t_ref[0] = jnp.sin(ang_t)


def _rope_tables(s, head_dim):
    half = head_dim // 2
    nb = s // MOBA_BLOCK
    inv = ROPE_THETA ** (-jnp.arange(half, dtype=_F32) / half)
    inv_row = jnp.tile(inv, LANES // half).reshape(1, LANES)
    sign_row = jnp.tile(jnp.concatenate([-jnp.ones(half, _F32), jnp.ones(half, _F32)]),
                        LANES // head_dim).reshape(1, LANES)
    inv_col = jnp.broadcast_to(inv[:, None], (half, MOBA_BLOCK))
    return pl.pallas_call(
        _rope_table_kernel,
        grid=(nb,),
        in_specs=[pl.BlockSpec((1, LANES), lambda t: (0, 0)), pl.BlockSpec((1, LANES), lambda t: (0, 0)),
                  pl.BlockSpec((half, MOBA_BLOCK), lambda t: (0, 0))],
        out_specs=[pl.BlockSpec((MOBA_BLOCK, LANES), lambda t: (t, 0)),
                   pl.BlockSpec((MOBA_BLOCK, LANES), lambda t: (t, 0)),
                   pl.BlockSpec((1, half, MOBA_BLOCK), lambda t: (t, 0, 0)),
                   pl.BlockSpec((1, half, MOBA_BLOCK), lambda t: (t, 0, 0))],
        out_shape=[jax.ShapeDtypeStruct((s, LANES), _F32), jax.ShapeDtypeStruct((s, LANES), _F32),
                   jax.ShapeDtypeStruct((nb, half, MOBA_BLOCK), _F32),
                   jax.ShapeDtypeStruct((nb, half, MOBA_BLOCK), _F32)],
        name="rope_tables",
    )(inv_row, sign_row, inv_col)


def _qkv_kernel(x_ref, g_ref, wqt_ref, wk_ref, wvt_ref, cos_k_ref, sin_k_ref, cos_t_ref, sin_t_ref,
                qt_ref, k_ref, vt_ref, kmean_ref, *, head_dim):
    x = x_ref[0]
    d = x.shape[-1]
    half = head_dim // 2
    h = _rmsnorm(x, g_ref[...]).astype(_BF16)
    nt = (((1,), (1,)), ((), ()))

    qt = lax.dot_general(wqt_ref[...], h, nt, preferred_element_type=_F32)
    cos_t, sin_t = cos_t_ref[0], sin_t_ref[0]
    q_scale = head_dim ** -0.5 * LOG2E
    for lo in range(0, d, head_dim):
        x1, x2 = qt[lo:lo + half], qt[lo + half:lo + head_dim]
        qt_ref[0, 0, lo:lo + half, :] = ((x1 * cos_t - x2 * sin_t) * q_scale).astype(_BF16)
        qt_ref[0, 0, lo + half:lo + head_dim, :] = ((x2 * cos_t + x1 * sin_t) * q_scale).astype(_BF16)

    vt = lax.dot_general(wvt_ref[...], h, nt, preferred_element_type=_F32).astype(_BF16)
    v_rows = head_dim + ONES_ROWS
    for hd in range(d // head_dim):
        vt_ref[0, 0, hd * v_rows:hd * v_rows + head_dim, :] = vt[hd * head_dim:(hd + 1) * head_dim]
        vt_ref[0, 0, hd * v_rows + head_dim:(hd + 1) * v_rows, :] = jnp.ones((ONES_ROWS, MOBA_BLOCK), _BF16)

    cos_k, sin_k = cos_k_ref[...], sin_k_ref[...]
    lane = lax.broadcasted_iota(jnp.int32, cos_k.shape, 1)
    first_half = (lane % head_dim) < half
    for lo in range(0, d, LANES):
        kk = _dot(h, wk_ref[:, lo:lo + LANES])
        partner = jnp.where(first_half, pltpu.roll(kk, LANES - half, axis=1), pltpu.roll(kk, half, axis=1))
        roped = kk * cos_k + partner * sin_k
        k_ref[0, :, lo:lo + LANES] = roped.astype(_BF16)
        kmean_ref[0, 0, :, lo:lo + LANES] = jnp.mean(roped, axis=0, keepdims=True)


def _qkv(x, g, w_qkv, tables):
    bsz, s, d = x.shape
    head_dim = d // N_HEADS
    half = head_dim // 2
    nb = s // MOBA_BLOCK
    wq, wk, wv = jnp.split(w_qkv.astype(_BF16), 3, axis=1)
    cos_k, sin_k, cos_t, sin_t = tables
    tile = pl.BlockSpec((1, MOBA_BLOCK, d), lambda b, t: (b, t, 0))
    tposed = pl.BlockSpec((1, 1, d, MOBA_BLOCK), lambda b, t: (b, t, 0, 0))
    d_aug = N_HEADS * (head_dim + ONES_ROWS)
    v_tposed = pl.BlockSpec((1, 1, d_aug, MOBA_BLOCK), lambda b, t: (b, t, 0, 0))
    k_table = pl.BlockSpec((MOBA_BLOCK, LANES), lambda b, t: (t, 0))
    t_table = pl.BlockSpec((1, half, MOBA_BLOCK), lambda b, t: (t, 0, 0))
    return pl.pallas_call(
        functools.partial(_qkv_kernel, head_dim=head_dim),
        grid=(bsz, nb),
        in_specs=[tile, _const_spec((1, d)), _const_spec((d, d)), _const_spec((d, d)), _const_spec((d, d)),
                  k_table, k_table, t_table, t_table],
        out_specs=[tposed, tile, v_tposed, pl.BlockSpec((1, 1, 1, d), lambda b, t: (b, t, 0, 0))],
        out_shape=[jax.ShapeDtypeStruct((bsz, nb, d, MOBA_BLOCK), _BF16),
                   jax.ShapeDtypeStruct((bsz, s, d), _BF16),
                   jax.ShapeDtypeStruct((bsz, nb, d_aug, MOBA_BLOCK), _BF16),
                   jax.ShapeDtypeStruct((bsz, nb, 1, d), _F32)],
        compiler_params=pltpu.CompilerParams(
            dimension_semantics=("arbitrary", "arbitrary"), vmem_limit_bytes=VMEM_LIMIT),
        name="qkv_rope",
    )(x, g.reshape(1, d), wq.T, wk, wv.T, cos_k, sin_k, cos_t, sin_t)


def _moba_kernel(qt_ref, k_ref, vt_ref, kmean_ref, o_ref, q_ref, bias_ref, acc_ref, *, head_dim):
    own = pl.program_id(2)
    nb = kmean_ref.shape[1]
    blk = MOBA_BLOCK
    qt_pair = qt_ref[0, 0]
    kmean = kmean_ref[0].astype(_BF16)
    feat = lax.broadcasted_iota(jnp.int32, qt_pair.shape, 0)
    block_id = lax.broadcasted_iota(jnp.int32, (nb, blk), 0)
    key_pos = lax.broadcasted_iota(jnp.int32, (blk, blk), 0)
    query_pos = lax.broadcasted_iota(jnp.int32, (blk, blk), 1)
    k_own = k_ref[0, pl.ds(pl.multiple_of(own * blk, blk), blk), :]

    v_rows = head_dim + ONES_ROWS
    causal = key_pos <= query_pos

    def k_block(j):
        return k_ref[0, pl.ds(pl.multiple_of(j * blk, blk), blk), :]

    def v_block(j, hh):
        return vt_ref[0, j, hh * v_rows:(hh + 1) * v_rows, :]

    q_heads, own_peaks = [], []
    for hh in range(HEADS_PER_STEP):
        q_h = jnp.where((feat >= hh * head_dim) & (feat < (hh + 1) * head_dim), qt_pair, jnp.zeros_like(qt_pair))
        q_heads.append(q_h)
        cols = slice(hh * blk, (hh + 1) * blk)
        q_ref[:, cols] = q_h

        gate = jnp.where(block_id < own, _dot(kmean, q_h), NEG_INF)
        bias = jnp.full((nb, blk), NEG_INF, _F32)
        for _ in range(MOBA_TOPK):
            best = jnp.max(gate, axis=0, keepdims=True)
            first = jnp.min(jnp.where(gate == best, block_id, nb), axis=0, keepdims=True)
            picked = block_id == first
            bias = jnp.where(picked, 0.0, bias)
            gate = jnp.where(picked, BELOW_NEG_INF, gate)

        bias_ref[:, cols] = jnp.where(block_id < own, bias, NEG_INF)

        s = jnp.where(causal, _dot(k_own, q_h), NEG_INF)
        own_peaks.append(jnp.max(s.reshape(blk // 8, 8, blk), axis=0))
        acc_ref[hh] = _dot(v_block(own, hh), jnp.exp2(s).astype(_BF16))

    def visit_groups(first_block, group, n_groups, peak):
        def visit(i, peak):
            j0 = first_block + i * group
            keys = k_ref[0, pl.ds(pl.multiple_of(j0 * blk, blk), group * blk), :]
            s = _dot(keys, q_ref[...])
            updates = [None] * HEADS_PER_STEP
            for g in range(group):
                t = s[g * blk:(g + 1) * blk] + bias_ref[pl.ds(j0 + g, 1), :]
                peak = jnp.maximum(peak, jnp.max(t.reshape(blk // 8, 8, HEADS_PER_STEP * blk), axis=0))
                p = jnp.exp2(t).astype(_BF16)
                for hh in range(HEADS_PER_STEP):
                    u = _dot(v_block(j0 + g, hh), p[:, hh * blk:(hh + 1) * blk])
                    updates[hh] = u if updates[hh] is None else updates[hh] + u
            for hh in range(HEADS_PER_STEP):
                acc_ref[hh] += updates[hh]
            return peak

        return lax.fori_loop(0, n_groups, visit, peak)

    n_large = own // BLOCKS_PER_VISIT
    n_small = (own - n_large * BLOCKS_PER_VISIT + TAIL_BLOCKS_PER_VISIT - 1) // TAIL_BLOCKS_PER_VISIT
    peak = visit_groups(0, BLOCKS_PER_VISIT, n_large, jnp.concatenate(own_peaks, axis=1))
    peak = visit_groups(n_large * BLOCKS_PER_VISIT, TAIL_BLOCKS_PER_VISIT, n_small, peak)
    query_peak = jnp.max(peak, axis=0, keepdims=True)
    out_of_range = (jnp.max(query_peak) > EXPONENT_GUARD) | (jnp.min(query_peak) < -EXPONENT_GUARD)

    @pl.when(out_of_range)
    def _():
        for hh in range(HEADS_PER_STEP):
            s = jnp.where(causal, _dot(k_own, q_heads[hh]), NEG_INF)
            m_own = jnp.max(s, axis=0, keepdims=True)
            acc = _dot(v_block(own, hh), jnp.exp2(s - m_own).astype(_BF16))

            def visit_exact(j, carry, hh=hh):
                m_run, acc = carry
                t = _dot(k_block(j), q_heads[hh]) + bias_ref[pl.ds(j, 1), hh * blk:(hh + 1) * blk]
                m_new = jnp.maximum(m_run, jnp.max(t, axis=0, keepdims=True))
                acc = jnp.exp2(m_run - m_new) * acc + _dot(v_block(j, hh), jnp.exp2(t - m_new).astype(_BF16))
                return m_new, acc

            _, acc = lax.fori_loop(0, own, visit_exact, (m_own, acc))
            acc_ref[hh] = acc

    o_t = jnp.concatenate([acc_ref[hh, :head_dim] / acc_ref[hh, head_dim:head_dim + 1]
                           for hh in range(HEADS_PER_STEP)], axis=0)
    o_ref[0] = o_t.T.astype(_BF16)


def _moba(qt, k, vt, kmean):
    bsz, s, d = k.shape
    head_dim = d // N_HEADS
    nb = s // MOBA_BLOCK
    width = HEADS_PER_STEP * head_dim
    v_rows = head_dim + ONES_ROWS
    return pl.pallas_call(
        functools.partial(_moba_kernel, head_dim=head_dim),
        grid=(bsz, d // width, nb),
        in_specs=[pl.BlockSpec((1, 1, width, MOBA_BLOCK), lambda b, c, i: (b, i, c, 0)),
                  pl.BlockSpec((1, s, width), lambda b, c, i: (b, 0, c)),
                  pl.BlockSpec((1, nb, HEADS_PER_STEP * v_rows, MOBA_BLOCK), lambda b, c, i: (b, 0, c, 0)),
                  pl.BlockSpec((1, nb, width), lambda b, c, i: (b, 0, c))],
        out_specs=pl.BlockSpec((1, MOBA_BLOCK, width), lambda b, c, i: (b, i, c)),
        out_shape=jax.ShapeDtypeStruct((bsz, s, d), _BF16),
        scratch_shapes=[pltpu.VMEM((width, HEADS_PER_STEP * MOBA_BLOCK), _BF16),
                        pltpu.VMEM((nb, HEADS_PER_STEP * MOBA_BLOCK), _F32),
                        pltpu.VMEM((HEADS_PER_STEP, v_rows, MOBA_BLOCK), _F32)],
        compiler_params=pltpu.CompilerParams(
            dimension_semantics=("arbitrary", "arbitrary", "arbitrary"), vmem_limit_bytes=VMEM_LIMIT),
        name="moba_attention",
    )(qt, k, vt, kmean.reshape(bsz, nb, d))


def kernel(x, mix_norm, sc_w_in, sc_w_conv, sc_w_out, moba_w_qkv, moba_w_o, ffn_norm, ffn_w_up, ffn_w_conv,
           ffn_w_down, final_norm):
    bsz, s, d = x.shape
    assert N_HEADS % HEADS_PER_STEP == 0 and (d // N_HEADS * HEADS_PER_STEP) % LANES == 0
    assert s % (MOBA_BLOCK * BLOCKS_PER_VISIT) == 0 and s % min(TM_FFN, s) == 0 and s % min(TM_MIXER, s) == 0
    assert mix_norm.shape[0] == 2 and ffn_w_down.shape[1] % CK_FFN == 0 and d % CK_MIXER == 0

    x = _mixer(x, mix_norm[0], sc_w_in[0], sc_w_conv[0], sc_w_out[0])
    x = _ffn(x, ffn_norm[0], ffn_w_up[0], ffn_w_conv[0], ffn_w_down[0])
    qt, k, vt, kmean = _qkv(x, mix_norm[1], moba_w_qkv[0], _rope_tables(s, d // N_HEADS))
    attn = _moba(qt, k, vt, kmean)
    return _ffn(x, ffn_norm[1], ffn_w_up[1], ffn_w_conv[1], ffn_w_down[1],
                attn=attn, w_o=moba_w_o[0], final_g=final_norm)
```

```python
import functools

import jax
import jax.numpy as jnp
from jax import lax
from jax.experimental import pallas as pl
from jax.experimental.pallas import tpu as pltpu

N_HEADS = 16
MOBA_BLOCK = 256
MOBA_TOPK = 3
ROPE_THETA = 10000.0
RMS_EPS = 1e-6
NEG_INF = -1e30
BELOW_NEG_INF = -3.0e38
LOG2E = 1.4426950408889634
CARRY_ROWS = 8
LANES = 128
HEADS_PER_STEP = 4
ONES_ROWS = 16
BLOCKS_PER_VISIT = 8
TAIL_BLOCKS_PER_VISIT = 4
EXPONENT_GUARD = 64.0

TM_MIXER = 512
CK_MIXER = 512
TM_FFN = 1024
SUB_FFN = 256
CK_FFN = 256
VMEM_LIMIT = 56 * 1024 * 1024

_BF16 = jnp.bfloat16
_F32 = jnp.float32


def _dot(a, b):
    return jnp.dot(a, b, preferred_element_type=_F32)


def _rmsnorm(x, g):
    var = jnp.mean(x * x, axis=-1, keepdims=True)
    return x * lax.rsqrt(var + RMS_EPS) * g


def _delay_rows(u, prev, n):
    rolled = pltpu.roll(u, n, axis=0)
    tail = pltpu.roll(prev, n, axis=0)
    row = lax.broadcasted_iota(jnp.int32, prev.shape, 0)
    first = jnp.where(row < n, tail, rolled[:CARRY_ROWS])
    return jnp.concatenate([first, rolled[CARRY_ROWS:]], axis=0)


def _causal_conv3(u, prev, w):
    return _delay_rows(u, prev, 2) * w[0:1] + _delay_rows(u, prev, 1) * w[1:2] + u * w[2:3]


def _reset_carry_at_sequence_start(carry_ref):
    @pl.when(pl.program_id(1) == 0)
    def _():
        carry_ref[...] = jnp.zeros_like(carry_ref)


def _mixer_kernel(x_ref, g_ref, win_ref, wconv_ref, wout_ref, o_ref, carry_ref, *, ck):
    _reset_carry_at_sequence_start(carry_ref)
    x = x_ref[0]
    d = x.shape[-1]
    h = _rmsnorm(x, g_ref[...]).astype(_BF16)
    acc = x
    for lo in range(0, d, ck):
        bb = _dot(h, win_ref[:, lo:lo + ck])
        cv = _dot(h, win_ref[:, d + lo:d + lo + ck]) * _dot(h, win_ref[:, 2 * d + lo:2 * d + lo + ck])
        y = _causal_conv3(cv, carry_ref[:, lo:lo + ck], wconv_ref[:, lo:lo + ck])
        carry_ref[:, lo:lo + ck] = cv[-CARRY_ROWS:, :]
        acc = acc + _dot((bb * y).astype(_BF16), wout_ref[lo:lo + ck, :])
    o_ref[0] = acc


def _const_spec(shape):
    zeros = (0,) * len(shape)
    return pl.BlockSpec(shape, lambda *_: zeros, pipeline_mode=pl.Buffered(1))


def _mixer(x, g, w_in, w_conv, w_out):
    bsz, s, d = x.shape
    tm = min(TM_MIXER, s)
    tile = pl.BlockSpec((1, tm, d), lambda b, t: (b, t, 0))
    return pl.pallas_call(
        functools.partial(_mixer_kernel, ck=CK_MIXER),
        grid=(bsz, s // tm),
        in_specs=[tile, _const_spec((1, d)), _const_spec((d, 3 * d)), _const_spec((3, d)), _const_spec((d, d))],
        out_specs=tile,
        out_shape=jax.ShapeDtypeStruct(x.shape, _F32),
        scratch_shapes=[pltpu.VMEM((CARRY_ROWS, d), _F32)],
        compiler_params=pltpu.CompilerParams(
            dimension_semantics=("arbitrary", "arbitrary"), vmem_limit_bytes=VMEM_LIMIT),
        name="mixer",
    )(x, g.reshape(1, d), w_in.astype(_BF16), w_conv, w_out.astype(_BF16))


def _ffn_body(x, g_ref, wup_ref, wconv_ref, wdown_ref, carry_ref, *, d_ff, ck):
    h = _rmsnorm(x, g_ref[...]).astype(_BF16)

    def up_project(lo):
        halves = []
        for col in (lo, d_ff + lo):
            up = _dot(h, wup_ref[:, col:col + ck])
            halves.append((up, carry_ref[:, col:col + ck]))
            carry_ref[:, col:col + ck] = up[-CARRY_ROWS:, :]
        return halves

    chunks = list(range(0, d_ff, ck))
    acc = x
    ahead = up_project(chunks[0])
    for c, lo in enumerate(chunks):
        gate, lin = (_causal_conv3(up, prev, wconv_ref[:, col:col + ck])
                     for (up, prev), col in zip(ahead, (lo, d_ff + lo)))
        act = (gate / (1.0 + jnp.exp(-gate)) * lin).astype(_BF16)
        if c + 1 < len(chunks):
            ahead = up_project(chunks[c + 1])
        acc = acc + _dot(act, wdown_ref[lo:lo + ck, :])
    return acc


def _ffn_kernel(x_ref, g_ref, wup_ref, wconv_ref, wdown_ref, o_ref, carry_ref, *, d_ff, ck):
    _reset_carry_at_sequence_start(carry_ref)
    for lo in range(0, x_ref.shape[1], SUB_FFN):
        rows = slice(lo, lo + SUB_FFN)
        o_ref[0, rows] = _ffn_body(x_ref[0, rows], g_ref, wup_ref, wconv_ref, wdown_ref, carry_ref, d_ff=d_ff, ck=ck)


def _proj_ffn_norm_kernel(x_ref, a_ref, wo_ref, g_ref, wup_ref, wconv_ref, wdown_ref, fg_ref, o_ref, carry_ref,
                          *, d_ff, ck):
    _reset_carry_at_sequence_start(carry_ref)
    for lo in range(0, x_ref.shape[1], SUB_FFN):
        rows = slice(lo, lo + SUB_FFN)
        x = x_ref[0, rows] + _dot(a_ref[0, rows], wo_ref[...])
        y = _ffn_body(x, g_ref, wup_ref, wconv_ref, wdown_ref, carry_ref, d_ff=d_ff, ck=ck)
        o_ref[0, rows] = _rmsnorm(y, fg_ref[...])


def _ffn(x, g, w_up, w_conv, w_down, attn=None, w_o=None, final_g=None):
    bsz, s, d = x.shape
    d_ff = w_down.shape[0]
    tm = min(TM_FFN, s)
    tile = pl.BlockSpec((1, tm, d), lambda b, t: (b, t, 0))
    ffn_specs = [_const_spec((1, d)), _const_spec((d, 2 * d_ff)), _const_spec((3, 2 * d_ff)), _const_spec((d_ff, d))]
    ffn_args = [g.reshape(1, d), w_up.astype(_BF16), w_conv, w_down.astype(_BF16)]
    if attn is None:
        body, name = _ffn_kernel, "ffn"
        in_specs, args = [tile] + ffn_specs, [x] + ffn_args
    else:
        body, name = _proj_ffn_norm_kernel, "proj_ffn_norm"
        in_specs = [tile, tile, _const_spec((d, d))] + ffn_specs + [_const_spec((1, d))]
        args = [x, attn, w_o.astype(_BF16)] + ffn_args + [final_g.reshape(1, d)]
    return pl.pallas_call(
        functools.partial(body, d_ff=d_ff, ck=CK_FFN),
        grid=(bsz, s // tm),
        in_specs=in_specs,
        out_specs=tile,
        out_shape=jax.ShapeDtypeStruct(x.shape, _F32),
        scratch_shapes=[pltpu.VMEM((CARRY_ROWS, 2 * d_ff), _F32)],
        compiler_params=pltpu.CompilerParams(
            dimension_semantics=("arbitrary", "arbitrary"), vmem_limit_bytes=VMEM_LIMIT),
        name=name,
    )(*args)


def _rope_table_kernel(inv_row_ref, sign_row_ref, inv_col_ref, cos_k_ref, sin_k_ref, cos_t_ref, sin_t_ref):
    base = pl.program_id(0) * MOBA_BLOCK
    pos_rows = (base + lax.broadcasted_iota(jnp.int32, cos_k_ref.shape, 0)).astype(_F32)
    ang = pos_rows * inv_row_ref[...]
    cos_k_ref[...] = jnp.cos(ang)
    sin_k_ref[...] = jnp.sin(ang) * sign_row_ref[...]
    pos_cols = (base + lax.broadcasted_iota(jnp.int32, cos_t_ref.shape[1:], 1)).astype(_F32)
    ang_t = pos_cols * inv_col_ref[...]
    cos_t_ref[0] = jnp.cos(ang_t)
    sin_t_ref[0] = jnp.sin(ang_t)


def _rope_tables(s, head_dim):
    half = head_dim // 2
    nb = s // MOBA_BLOCK
    inv = ROPE_THETA ** (-jnp.arange(half, dtype=_F32) / half)
    inv_row = jnp.tile(inv, LANES // half).reshape(1, LANES)
    sign_row = jnp.tile(jnp.concatenate([-jnp.ones(half, _F32), jnp.ones(half, _F32)]),
                        LANES // head_dim).reshape(1, LANES)
    inv_col = jnp.broadcast_to(inv[:, None], (half, MOBA_BLOCK))
    return pl.pallas_call(
        _rope_table_kernel,
        grid=(nb,),
        in_specs=[pl.BlockSpec((1, LANES), lambda t: (0, 0)), pl.BlockSpec((1, LANES), lambda t: (0, 0)),
                  pl.BlockSpec((half, MOBA_BLOCK), lambda t: (0, 0))],
        out_specs=[pl.BlockSpec((MOBA_BLOCK, LANES), lambda t: (t, 0)),
                   pl.BlockSpec((MOBA_BLOCK, LANES), lambda t: (t, 0)),
                   pl.BlockSpec((1, half, MOBA_BLOCK), lambda t: (t, 0, 0)),
                   pl.BlockSpec((1, half, MOBA_BLOCK), lambda t: (t, 0, 0))],
        out_shape=[jax.ShapeDtypeStruct((s, LANES), _F32), jax.ShapeDtypeStruct((s, LANES), _F32),
                   jax.ShapeDtypeStruct((nb, half, MOBA_BLOCK), _F32),
                   jax.ShapeDtypeStruct((nb, half, MOBA_BLOCK), _F32)],
        name="rope_tables",
    )(inv_row, sign_row, inv_col)


def _qkv_kernel(x_ref, g_ref, wqt_ref, wk_ref, wvt_ref, cos_k_ref, sin_k_ref, cos_t_ref, sin_t_ref,
                qt_ref, k_ref, vt_ref, kmean_ref, *, head_dim):
    x = x_ref[0]
    d = x.shape[-1]
    half = head_dim // 2
    h = _rmsnorm(x, g_ref[...]).astype(_BF16)
    nt = (((1,), (1,)), ((), ()))

    qt = lax.dot_general(wqt_ref[...], h, nt, preferred_element_type=_F32)
    cos_t, sin_t = cos_t_ref[0], sin_t_ref[0]
    q_scale = head_dim ** -0.5 * LOG2E
    for lo in range(0, d, head_dim):
        x1, x2 = qt[lo:lo + half], qt[lo + half:lo + head_dim]
        qt_ref[0, 0, lo:lo + half, :] = ((x1 * cos_t - x2 * sin_t) * q_scale).astype(_BF16)
        qt_ref[0, 0, lo + half:lo + head_dim, :] = ((x2 * cos_t + x1 * sin_t) * q_scale).astype(_BF16)

    vt = lax.dot_general(wvt_ref[...], h, nt, preferred_element_type=_F32).astype(_BF16)
    v_rows = head_dim + ONES_ROWS
    for hd in range(d // head_dim):
        vt_ref[0, 0, hd * v_rows:hd * v_rows + head_dim, :] = vt[hd * head_dim:(hd + 1) * head_dim]
        vt_ref[0, 0, hd * v_rows + head_dim:(hd + 1) * v_rows, :] = jnp.ones((ONES_ROWS, MOBA_BLOCK), _BF16)

    cos_k, sin_k = cos_k_ref[...], sin_k_ref[...]
    lane = lax.broadcasted_iota(jnp.int32, cos_k.shape, 1)
    first_half = (lane % head_dim) < half
    for lo in range(0, d, LANES):
        kk = _dot(h, wk_ref[:, lo:lo + LANES])
        partner = jnp.where(first_half, pltpu.roll(kk, LANES - half, axis=1), pltpu.roll(kk, half, axis=1))
        roped = kk * cos_k + partner * sin_k
        k_ref[0, :, lo:lo + LANES] = roped.astype(_BF16)
        kmean_ref[0, 0, :, lo:lo + LANES] = jnp.mean(roped, axis=0, keepdims=True)


def _qkv(x, g, w_qkv, tables):
    bsz, s, d = x.shape
    head_dim = d // N_HEADS
    half = head_dim // 2
    nb = s // MOBA_BLOCK
    wq, wk, wv = jnp.split(w_qkv.astype(_BF16), 3, axis=1)
    cos_k, sin_k, cos_t, sin_t = tables
    tile = pl.BlockSpec((1, MOBA_BLOCK, d), lambda b, t: (b, t, 0))
    tposed = pl.BlockSpec((1, 1, d, MOBA_BLOCK), lambda b, t: (b, t, 0, 0))
    d_aug = N_HEADS * (head_dim + ONES_ROWS)
    v_tposed = pl.BlockSpec((1, 1, d_aug, MOBA_BLOCK), lambda b, t: (b, t, 0, 0))
    k_table = pl.BlockSpec((MOBA_BLOCK, LANES), lambda b, t: (t, 0))
    t_table = pl.BlockSpec((1, half, MOBA_BLOCK), lambda b, t: (t, 0, 0))
    return pl.pallas_call(
        functools.partial(_qkv_kernel, head_dim=head_dim),
        grid=(bsz, nb),
        in_specs=[tile, _const_spec((1, d)), _const_spec((d, d)), _const_spec((d, d)), _const_spec((d, d)),
                  k_table, k_table, t_table, t_table],
        out_specs=[tposed, tile, v_tposed, pl.BlockSpec((1, 1, 1, d), lambda b, t: (b, t, 0, 0))],
        out_shape=[jax.ShapeDtypeStruct((bsz, nb, d, MOBA_BLOCK), _BF16),
                   jax.ShapeDtypeStruct((bsz, s, d), _BF16),
                   jax.ShapeDtypeStruct((bsz, nb, d_aug, MOBA_BLOCK), _BF16),
                   jax.ShapeDtypeStruct((bsz, nb, 1, d), _F32)],
        compiler_params=pltpu.CompilerParams(
            dimension_semantics=("arbitrary", "arbitrary"), vmem_limit_bytes=VMEM_LIMIT),
        name="qkv_rope",
    )(x, g.reshape(1, d), wq.T, wk, wv.T, cos_k, sin_k, cos_t, sin_t)


def _moba_kernel(qt_ref, k_ref, vt_ref, kmean_ref, causal_ref, o_ref, q_ref, bias_ref, acc_ref, *, head_dim):
    own = pl.program_id(2)
    nb = kmean_ref.shape[1]
    blk = MOBA_BLOCK
    width = HEADS_PER_STEP * blk
    v_rows = head_dim + ONES_ROWS

    def head_cols(hh):
        return slice(hh * blk, (hh + 1) * blk)

    def k_block(j):
        return k_ref[0, pl.ds(pl.multiple_of(j * blk, blk), blk), :]

    def v_block(j, hh):
        return vt_ref[0, j, hh * v_rows:(hh + 1) * v_rows, :]

    q_ref[...] = jnp.zeros_like(q_ref)
    for hh in range(HEADS_PER_STEP):
        feats = slice(hh * head_dim, (hh + 1) * head_dim)
        q_ref[feats, head_cols(hh)] = qt_ref[0, 0, feats, :]
    q_all = q_ref[...]

    block_id = lax.broadcasted_iota(jnp.int32, (nb, width), 0)
    gate = jnp.where(block_id < own, _dot(kmean_ref[0].astype(_BF16), q_all), NEG_INF)
    bias = jnp.full((nb, width), NEG_INF, _F32)
    for _ in range(MOBA_TOPK):
        best = jnp.max(gate, axis=0, keepdims=True)
        first = jnp.min(jnp.where(gate == best, block_id, nb), axis=0, keepdims=True)
        picked = block_id == first
        bias = jnp.where(picked, 0.0, bias)
        gate = jnp.where(picked, BELOW_NEG_INF, gate)
    bias_ref[...] = jnp.where(block_id < own, bias, NEG_INF)

    s_own = _dot(k_block(own), q_all) + causal_ref[...]
    own_peak = jnp.max(s_own.reshape(blk // 8, 8, width), axis=0)
    p_own = jnp.exp2(s_own).astype(_BF16)
    for hh in range(HEADS_PER_STEP):
        acc_ref[hh] = _dot(v_block(own, hh), p_own[:, head_cols(hh)])

    def visit_groups(first_block, group, n_groups, peak):
        def visit(i, peak):
            j0 = first_block + i * group
            keys = k_ref[0, pl.ds(pl.multiple_of(j0 * blk, blk), group * blk), :]
            s = _dot(keys, q_ref[...])
            updates = [None] * HEADS_PER_STEP
            for g in range(group):
                t = s[g * blk:(g + 1) * blk] + bias_ref[pl.ds(j0 + g, 1), :]
                peak = jnp.maximum(peak, jnp.max(t.reshape(blk // 8, 8, width), axis=0))
                p = jnp.exp2(t).astype(_BF16)
                for hh in range(HEADS_PER_STEP):
                    u = _dot(v_block(j0 + g, hh), p[:, head_cols(hh)])
                    updates[hh] = u if updates[hh] is None else updates[hh] + u
            for hh in range(HEADS_PER_STEP):
                acc_ref[hh] += updates[hh]
            return peak

        return lax.fori_loop(0, n_groups, visit, peak)

    n_large = own // BLOCKS_PER_VISIT
    n_small = (own - n_large * BLOCKS_PER_VISIT + TAIL_BLOCKS_PER_VISIT - 1) // TAIL_BLOCKS_PER_VISIT
    peak = visit_groups(0, BLOCKS_PER_VISIT, n_large, own_peak)
    peak = visit_groups(n_large * BLOCKS_PER_VISIT, TAIL_BLOCKS_PER_VISIT, n_small, peak)
    query_peak = jnp.max(peak, axis=0, keepdims=True)
    out_of_range = (jnp.max(query_peak) > EXPONENT_GUARD) | (jnp.min(query_peak) < -EXPONENT_GUARD)

    @pl.when(out_of_range)
    def _():
        for hh in range(HEADS_PER_STEP):
            s = _dot(k_block(own), q_ref[:, head_cols(hh)]) + causal_ref[:, head_cols(hh)]
            m_own = jnp.max(s, axis=0, keepdims=True)
            acc = _dot(v_block(own, hh), jnp.exp2(s - m_own).astype(_BF16))

            def visit_exact(j, carry, hh=hh):
                m_run, acc = carry
                t = _dot(k_block(j), q_ref[:, head_cols(hh)]) + bias_ref[pl.ds(j, 1), head_cols(hh)]
                m_new = jnp.maximum(m_run, jnp.max(t, axis=0, keepdims=True))
                acc = jnp.exp2(m_run - m_new) * acc + _dot(v_block(j, hh), jnp.exp2(t - m_new).astype(_BF16))
                return m_new, acc

            _, acc = lax.fori_loop(0, own, visit_exact, (m_own, acc))
            acc_ref[hh] = acc

    o_t = jnp.concatenate([acc_ref[hh, :head_dim] / acc_ref[hh, head_dim:head_dim + 1]
                           for hh in range(HEADS_PER_STEP)], axis=0)
    o_ref[0] = o_t.T.astype(_BF16)


def _moba(qt, k, vt, kmean):
    bsz, s, d = k.shape
    head_dim = d // N_HEADS
    nb = s // MOBA_BLOCK
    width = HEADS_PER_STEP * head_dim
    v_rows = head_dim + ONES_ROWS
    pos = jnp.arange(MOBA_BLOCK)
    causal = jnp.tile(jnp.where(pos[:, None] <= pos[None, :], 0.0, NEG_INF).astype(_F32), (1, HEADS_PER_STEP))
    return pl.pallas_call(
        functools.partial(_moba_kernel, head_dim=head_dim),
        grid=(bsz, d // width, nb),
        in_specs=[pl.BlockSpec((1, 1, width, MOBA_BLOCK), lambda b, c, i: (b, i, c, 0)),
                  pl.BlockSpec((1, s, width), lambda b, c, i: (b, 0, c)),
                  pl.BlockSpec((1, nb, HEADS_PER_STEP * v_rows, MOBA_BLOCK), lambda b, c, i: (b, 0, c, 0)),
                  pl.BlockSpec((1, nb, width), lambda b, c, i: (b, 0, c)),
                  pl.BlockSpec(causal.shape, lambda b, c, i: (0, 0))],
        out_specs=pl.BlockSpec((1, MOBA_BLOCK, width), lambda b, c, i: (b, i, c)),
        out_shape=jax.ShapeDtypeStruct((bsz, s, d), _BF16),
        scratch_shapes=[pltpu.VMEM((width, HEADS_PER_STEP * MOBA_BLOCK), _BF16),
                        pltpu.VMEM((nb, HEADS_PER_STEP * MOBA_BLOCK), _F32),
                        pltpu.VMEM((HEADS_PER_STEP, v_rows, MOBA_BLOCK), _F32)],
        compiler_params=pltpu.CompilerParams(
            dimension_semantics=("arbitrary", "arbitrary", "arbitrary"), vmem_limit_bytes=VMEM_LIMIT),
        name="moba_attention",
    )(qt, k, vt, kmean.reshape(bsz, nb, d), causal)


def kernel(x, mix_norm, sc_w_in, sc_w_conv, sc_w_out, moba_w_qkv, moba_w_o, ffn_norm, ffn_w_up, ffn_w_conv,
           ffn_w_down, final_norm):
    bsz, s, d = x.shape
    assert N_HEADS % HEADS_PER_STEP == 0 and (d // N_HEADS * HEADS_PER_STEP) % LANES == 0
    assert s % (MOBA_BLOCK * BLOCKS_PER_VISIT) == 0 and s % min(TM_FFN, s) == 0 and s % min(TM_MIXER, s) == 0
    assert mix_norm.shape[0] == 2 and ffn_w_down.shape[1] % CK_FFN == 0 and d % CK_MIXER == 0

    x = _mixer(x, mix_norm[0], sc_w_in[0], sc_w_conv[0], sc_w_out[0])
    x = _ffn(x, ffn_norm[0], ffn_w_up[0], ffn_w_conv[0], ffn_w_down[0])
    qt, k, vt, kmean = _qkv(x, mix_norm[1], moba_w_qkv[0], _rope_tables(s, d // N_HEADS))
    attn = _moba(qt, k, vt, kmean)
    return _ffn(x, ffn_norm[1], ffn_w_up[1], ffn_w_conv[1], ffn_w_down[1],
                attn=attn, w_o=moba_w_o[0], final_g=final_norm)
```

```python
import functools

import jax
import jax.numpy as jnp
from jax import lax
from jax.experimental import pallas as pl
from jax.experimental.pallas import tpu as pltpu

N_HEADS = 16
MOBA_BLOCK = 256
MOBA_TOPK = 3
ROPE_THETA = 10000.0
RMS_EPS = 1e-6
NEG_INF = -1e30
BELOW_NEG_INF = -3.0e38
LOG2E = 1.4426950408889634
CARRY_ROWS = 8
LANES = 128
MXU_COLS = 256
HEADS_PER_STEP = 4
ONES_ROWS = 16
BLOCKS_PER_VISIT = 8
TAIL_BLOCKS_PER_VISIT = 4
EXPONENT_GUARD = 64.0

QKV_BLOCKS = 4
TM_MIXER = 1024
SUB_MIXER = 512
CK_MIXER = 512
TM_FFN = 1024
SUB_FFN = 256
CK_FFN = 256
VMEM_LIMIT = 56 * 1024 * 1024

_BF16 = jnp.bfloat16
_F32 = jnp.float32


def _dot(a, b):
    return jnp.dot(a, b, preferred_element_type=_F32)


def _rmsnorm(x, g):
    var = jnp.mean(x * x, axis=-1, keepdims=True)
    return x * lax.rsqrt(var + RMS_EPS) * g


def _delay_rows(u, prev, n):
    rolled = pltpu.roll(u, n, axis=0)
    tail = pltpu.roll(prev, n, axis=0)
    row = lax.broadcasted_iota(jnp.int32, prev.shape, 0)
    first = jnp.where(row < n, tail, rolled[:CARRY_ROWS])
    return jnp.concatenate([first, rolled[CARRY_ROWS:]], axis=0)


def _causal_conv3(u, prev, w):
    return _delay_rows(u, prev, 2) * w[0:1] + _delay_rows(u, prev, 1) * w[1:2] + u * w[2:3]


def _reset_carry_at_sequence_start(carry_ref):
    @pl.when(pl.program_id(1) == 0)
    def _():
        carry_ref[...] = jnp.zeros_like(carry_ref)


def _mixer_kernel(x_ref, g_ref, win_ref, wconv_ref, wout_ref, o_ref, carry_ref, *, ck):
    _reset_carry_at_sequence_start(carry_ref)
    d = x_ref.shape[-1]
    for r0 in range(0, x_ref.shape[1], SUB_MIXER):
        rows = slice(r0, r0 + SUB_MIXER)
        x = x_ref[0, rows]
        h = _rmsnorm(x, g_ref[...]).astype(_BF16)
        acc = x
        for lo in range(0, d, ck):
            bb = _dot(h, win_ref[:, lo:lo + ck])
            cv = _dot(h, win_ref[:, d + lo:d + lo + ck]) * _dot(h, win_ref[:, 2 * d + lo:2 * d + lo + ck])
            y = _causal_conv3(cv, carry_ref[:, lo:lo + ck], wconv_ref[:, lo:lo + ck])
            carry_ref[:, lo:lo + ck] = cv[-CARRY_ROWS:, :]
            acc = acc + _dot((bb * y).astype(_BF16), wout_ref[lo:lo + ck, :])
        o_ref[0, rows] = acc


def _const_spec(shape):
    zeros = (0,) * len(shape)
    return pl.BlockSpec(shape, lambda *_: zeros, pipeline_mode=pl.Buffered(1))


def _mixer(x, g, w_in, w_conv, w_out):
    bsz, s, d = x.shape
    tm = min(TM_MIXER, s)
    tile = pl.BlockSpec((1, tm, d), lambda b, t: (b, t, 0))
    return pl.pallas_call(
        functools.partial(_mixer_kernel, ck=CK_MIXER),
        grid=(bsz, s // tm),
        in_specs=[tile, _const_spec((1, d)), _const_spec((d, 3 * d)), _const_spec((3, d)), _const_spec((d, d))],
        out_specs=tile,
        out_shape=jax.ShapeDtypeStruct(x.shape, _F32),
        scratch_shapes=[pltpu.VMEM((CARRY_ROWS, d), _F32)],
        compiler_params=pltpu.CompilerParams(
            dimension_semantics=("arbitrary", "arbitrary"), vmem_limit_bytes=VMEM_LIMIT),
        name="mixer",
    )(x, g.reshape(1, d), w_in.astype(_BF16), w_conv, w_out.astype(_BF16))


def _ffn_body(x, g_ref, wup_ref, wconv_ref, wdown_ref, carry_ref, *, d_ff, ck):
    h = _rmsnorm(x, g_ref[...]).astype(_BF16)

    def up_project(lo):
        halves = []
        for col in (lo, d_ff + lo):
            up = _dot(h, wup_ref[:, col:col + ck])
            halves.append((up, carry_ref[:, col:col + ck]))
            carry_ref[:, col:col + ck] = up[-CARRY_ROWS:, :]
        return halves

    chunks = list(range(0, d_ff, ck))
    acc = x
    ahead = up_project(chunks[0])
    for c, lo in enumerate(chunks):
        gate, lin = (_causal_conv3(up, prev, wconv_ref[:, col:col + ck])
                     for (up, prev), col in zip(ahead, (lo, d_ff + lo)))
        act = (gate / (1.0 + jnp.exp(-gate)) * lin).astype(_BF16)
        if c + 1 < len(chunks):
            ahead = up_project(chunks[c + 1])
        acc = acc + _dot(act, wdown_ref[lo:lo + ck, :])
    return acc


def _ffn_kernel(x_ref, g_ref, wup_ref, wconv_ref, wdown_ref, o_ref, carry_ref, *, d_ff, ck):
    _reset_carry_at_sequence_start(carry_ref)
    for lo in range(0, x_ref.shape[1], SUB_FFN):
        rows = slice(lo, lo + SUB_FFN)
        o_ref[0, rows] = _ffn_body(x_ref[0, rows], g_ref, wup_ref, wconv_ref, wdown_ref, carry_ref, d_ff=d_ff, ck=ck)


def _proj_ffn_norm_kernel(x_ref, a_ref, wo_ref, g_ref, wup_ref, wconv_ref, wdown_ref, fg_ref, o_ref, carry_ref,
                          *, d_ff, ck):
    _reset_carry_at_sequence_start(carry_ref)
    for lo in range(0, x_ref.shape[1], SUB_FFN):
        rows = slice(lo, lo + SUB_FFN)
        x = x_ref[0, rows] + _dot(a_ref[0, rows], wo_ref[...])
        y = _ffn_body(x, g_ref, wup_ref, wconv_ref, wdown_ref, carry_ref, d_ff=d_ff, ck=ck)
        o_ref[0, rows] = _rmsnorm(y, fg_ref[...])


def _ffn(x, g, w_up, w_conv, w_down, attn=None, w_o=None, final_g=None):
    bsz, s, d = x.shape
    d_ff = w_down.shape[0]
    tm = min(TM_FFN, s)
    tile = pl.BlockSpec((1, tm, d), lambda b, t: (b, t, 0))
    ffn_specs = [_const_spec((1, d)), _const_spec((d, 2 * d_ff)), _const_spec((3, 2 * d_ff)), _const_spec((d_ff, d))]
    ffn_args = [g.reshape(1, d), w_up.astype(_BF16), w_conv, w_down.astype(_BF16)]
    if attn is None:
        body, name = _ffn_kernel, "ffn"
        in_specs, args = [tile] + ffn_specs, [x] + ffn_args
    else:
        body, name = _proj_ffn_norm_kernel, "proj_ffn_norm"
        in_specs = [tile, tile, _const_spec((d, d))] + ffn_specs + [_const_spec((1, d))]
        args = [x, attn, w_o.astype(_BF16)] + ffn_args + [final_g.reshape(1, d)]
    return pl.pallas_call(
        functools.partial(body, d_ff=d_ff, ck=CK_FFN),
        grid=(bsz, s // tm),
        in_specs=in_specs,
        out_specs=tile,
        out_shape=jax.ShapeDtypeStruct(x.shape, _F32),
        scratch_shapes=[pltpu.VMEM((CARRY_ROWS, 2 * d_ff), _F32)],
        compiler_params=pltpu.CompilerParams(
            dimension_semantics=("arbitrary", "arbitrary"), vmem_limit_bytes=VMEM_LIMIT),
        name=name,
    )(*args)


def _rope_table_kernel(inv_row_ref, sign_row_ref, inv_col_ref, cos_k_ref, sin_k_ref, cos_t_ref, sin_t_ref):
    base = pl.program_id(0) * MOBA_BLOCK
    pos_rows = (base + lax.broadcasted_iota(jnp.int32, cos_k_ref.shape, 0)).astype(_F32)
    ang = pos_rows * inv_row_ref[...]
    cos_k_ref[...] = jnp.cos(ang)
    sin_k_ref[...] = jnp.sin(ang) * sign_row_ref[...]
    pos_cols = (base + lax.broadcasted_iota(jnp.int32, cos_t_ref.shape[1:], 1)).astype(_F32)
    ang_t = pos_cols * inv_col_ref[...]
    cos_t_ref[0] = jnp.cos(ang_t)
    sin_t_ref[0] = jnp.sin(ang_t)


def _rope_tables(s, head_dim):
    half = head_dim // 2
    nb = s // MOBA_BLOCK
    inv = ROPE_THETA ** (-jnp.arange(half, dtype=_F32) / half)
    inv_row = jnp.tile(inv, LANES // half).reshape(1, LANES)
    sign_row = jnp.tile(jnp.concatenate([-jnp.ones(half, _F32), jnp.ones(half, _F32)]),
                        LANES // head_dim).reshape(1, LANES)
    inv_col = jnp.broadcast_to(inv[:, None], (half, MOBA_BLOCK))
    return pl.pallas_call(
        _rope_table_kernel,
        grid=(nb,),
        in_specs=[pl.BlockSpec((1, LANES), lambda t: (0, 0)), pl.BlockSpec((1, LANES), lambda t: (0, 0)),
                  pl.BlockSpec((half, MOBA_BLOCK), lambda t: (0, 0))],
        out_specs=[pl.BlockSpec((MOBA_BLOCK, LANES), lambda t: (t, 0)),
                   pl.BlockSpec((MOBA_BLOCK, LANES), lambda t: (t, 0)),
                   pl.BlockSpec((1, half, MOBA_BLOCK), lambda t: (t, 0, 0)),
                   pl.BlockSpec((1, half, MOBA_BLOCK), lambda t: (t, 0, 0))],
        out_shape=[jax.ShapeDtypeStruct((s, LANES), _F32), jax.ShapeDtypeStruct((s, LANES), _F32),
                   jax.ShapeDtypeStruct((nb, half, MOBA_BLOCK), _F32),
                   jax.ShapeDtypeStruct((nb, half, MOBA_BLOCK), _F32)],
        name="rope_tables",
    )(inv_row, sign_row, inv_col)


def _qkv_kernel(x_ref, g_ref, wqt_ref, wk_ref, wvt_ref, cos_k_ref, sin_k_ref, cos_t_ref, sin_t_ref,
                qt_ref, k_ref, vt_ref, kmean_ref, *, head_dim):
    d = x_ref.shape[-1]
    half = head_dim // 2
    v_rows = head_dim + ONES_ROWS
    nt = (((1,), (1,)), ((), ()))
    q_scale = head_dim ** -0.5 * LOG2E
    lane = lax.broadcasted_iota(jnp.int32, (MOBA_BLOCK, LANES), 1)
    first_half = (lane % head_dim) < half

    for blk in range(qt_ref.shape[1]):
        rows = slice(blk * MOBA_BLOCK, (blk + 1) * MOBA_BLOCK)
        h = _rmsnorm(x_ref[0, rows], g_ref[...]).astype(_BF16)

        qt = lax.dot_general(wqt_ref[...], h, nt, preferred_element_type=_F32)
        cos_t, sin_t = cos_t_ref[blk], sin_t_ref[blk]
        for lo in range(0, d, head_dim):
            x1, x2 = qt[lo:lo + half], qt[lo + half:lo + head_dim]
            qt_ref[0, blk, lo:lo + half, :] = ((x1 * cos_t - x2 * sin_t) * q_scale).astype(_BF16)
            qt_ref[0, blk, lo + half:lo + head_dim, :] = ((x2 * cos_t + x1 * sin_t) * q_scale).astype(_BF16)

        vt = lax.dot_general(wvt_ref[...], h, nt, preferred_element_type=_F32).astype(_BF16)
        for hd in range(d // head_dim):
            vt_ref[0, blk, hd * v_rows:hd * v_rows + head_dim, :] = vt[hd * head_dim:(hd + 1) * head_dim]
            vt_ref[0, blk, hd * v_rows + head_dim:(hd + 1) * v_rows, :] = jnp.ones((ONES_ROWS, MOBA_BLOCK), _BF16)

        cos_k, sin_k = cos_k_ref[rows], sin_k_ref[rows]
        for lo in range(0, d, MXU_COLS):
            k_wide = _dot(h, wk_ref[:, lo:lo + MXU_COLS])
            for sub in range(lo, lo + MXU_COLS, LANES):
                kk = k_wide[:, sub - lo:sub - lo + LANES]
                partner = jnp.where(first_half, pltpu.roll(kk, LANES - half, axis=1), pltpu.roll(kk, half, axis=1))
                roped = kk * cos_k + partner * sin_k
                k_ref[0, rows, sub:sub + LANES] = roped.astype(_BF16)
                kmean_ref[0, blk, :, sub:sub + LANES] = jnp.mean(roped, axis=0, keepdims=True)


def _qkv(x, g, w_qkv, tables):
    bsz, s, d = x.shape
    head_dim = d // N_HEADS
    half = head_dim // 2
    nb = s // MOBA_BLOCK
    wq, wk, wv = jnp.split(w_qkv.astype(_BF16), 3, axis=1)
    cos_k, sin_k, cos_t, sin_t = tables
    nq = min(QKV_BLOCKS, nb)
    tile = pl.BlockSpec((1, nq * MOBA_BLOCK, d), lambda b, t: (b, t, 0))
    tposed = pl.BlockSpec((1, nq, d, MOBA_BLOCK), lambda b, t: (b, t, 0, 0))
    d_aug = N_HEADS * (head_dim + ONES_ROWS)
    v_tposed = pl.BlockSpec((1, nq, d_aug, MOBA_BLOCK), lambda b, t: (b, t, 0, 0))
    k_table = pl.BlockSpec((nq * MOBA_BLOCK, LANES), lambda b, t: (t, 0))
    t_table = pl.BlockSpec((nq, half, MOBA_BLOCK), lambda b, t: (t, 0, 0))
    return pl.pallas_call(
        functools.partial(_qkv_kernel, head_dim=head_dim),
        grid=(bsz, nb // nq),
        in_specs=[tile, _const_spec((1, d)), _const_spec((d, d)), _const_spec((d, d)), _const_spec((d, d)),
                  k_table, k_table, t_table, t_table],
        out_specs=[tposed, tile, v_tposed, pl.BlockSpec((1, nq, 1, d), lambda b, t: (b, t, 0, 0))],
        out_shape=[jax.ShapeDtypeStruct((bsz, nb, d, MOBA_BLOCK), _BF16),
                   jax.ShapeDtypeStruct((bsz, s, d), _BF16),
                   jax.ShapeDtypeStruct((bsz, nb, d_aug, MOBA_BLOCK), _BF16),
                   jax.ShapeDtypeStruct((bsz, nb, 1, d), _F32)],
        compiler_params=pltpu.CompilerParams(
            dimension_semantics=("arbitrary", "arbitrary"), vmem_limit_bytes=VMEM_LIMIT),
        name="qkv_rope",
    )(x, g.reshape(1, d), wq.T, wk, wv.T, cos_k, sin_k, cos_t, sin_t)


def _moba_kernel(qt_ref, k_ref, vt_ref, kmean_ref, causal_ref, o_ref, q_ref, bias_ref, acc_ref, *, head_dim):
    own = pl.program_id(2)
    nb = kmean_ref.shape[1]
    blk = MOBA_BLOCK
    width = HEADS_PER_STEP * blk
    v_rows = head_dim + ONES_ROWS

    def head_cols(hh):
        return slice(hh * blk, (hh + 1) * blk)

    def k_block(j):
        return k_ref[0, pl.ds(pl.multiple_of(j * blk, blk), blk), :]

    def v_block(j, hh):
        return vt_ref[0, j, hh * v_rows:(hh + 1) * v_rows, :]

    q_ref[...] = jnp.zeros_like(q_ref)
    for hh in range(HEADS_PER_STEP):
        feats = slice(hh * head_dim, (hh + 1) * head_dim)
        q_ref[feats, head_cols(hh)] = qt_ref[0, 0, feats, :]
    q_all = q_ref[...]

    block_id = lax.broadcasted_iota(jnp.int32, (nb, width), 0)
    gate = jnp.where(block_id < own, _dot(kmean_ref[0].astype(_BF16), q_all), NEG_INF)
    bias = jnp.full((nb, width), NEG_INF, _F32)
    for _ in range(MOBA_TOPK):
        best = jnp.max(gate, axis=0, keepdims=True)
        first = jnp.min(jnp.where(gate == best, block_id, nb), axis=0, keepdims=True)
        picked = block_id == first
        bias = jnp.where(picked, 0.0, bias)
        gate = jnp.where(picked, BELOW_NEG_INF, gate)
    bias_ref[...] = jnp.where(block_id < own, bias, NEG_INF)

    s_own = _dot(k_block(own), q_all) + causal_ref[...]
    own_peak = jnp.max(s_own.reshape(blk // 8, 8, width), axis=0)
    p_own = jnp.exp2(s_own).astype(_BF16)
    for hh in range(HEADS_PER_STEP):
        acc_ref[hh] = _dot(v_block(own, hh), p_own[:, head_cols(hh)])

    def visit_groups(first_block, group, n_groups, peak):
        def visit(i, peak):
            j0 = first_block + i * group
            keys = k_ref[0, pl.ds(pl.multiple_of(j0 * blk, blk), group * blk), :]
            s = _dot(keys, q_ref[...])
            updates = [None] * HEADS_PER_STEP
            for g in range(group):
                t = s[g * blk:(g + 1) * blk] + bias_ref[pl.ds(j0 + g, 1), :]
                peak = jnp.maximum(peak, jnp.max(t.reshape(blk // 8, 8, width), axis=0))
                p = jnp.exp2(t).astype(_BF16)
                for hh in range(HEADS_PER_STEP):
                    u = _dot(v_block(j0 + g, hh), p[:, head_cols(hh)])
                    updates[hh] = u if updates[hh] is None else updates[hh] + u
            for hh in range(HEADS_PER_STEP):
                acc_ref[hh] += updates[hh]
            return peak

        return lax.fori_loop(0, n_groups, visit, peak)

    n_large = own // BLOCKS_PER_VISIT
    n_small = (own - n_large * BLOCKS_PER_VISIT + TAIL_BLOCKS_PER_VISIT - 1) // TAIL_BLOCKS_PER_VISIT
    peak = visit_groups(0, BLOCKS_PER_VISIT, n_large, own_peak)
    peak = visit_groups(n_large * BLOCKS_PER_VISIT, TAIL_BLOCKS_PER_VISIT, n_small, peak)
    query_peak = jnp.max(peak, axis=0, keepdims=True)
    out_of_range = (jnp.max(query_peak) > EXPONENT_GUARD) | (jnp.min(query_peak) < -EXPONENT_GUARD)

    @pl.when(out_of_range)
    def _():
        for hh in range(HEADS_PER_STEP):
            s = _dot(k_block(own), q_ref[:, head_cols(hh)]) + causal_ref[:, head_cols(hh)]
            m_own = jnp.max(s, axis=0, keepdims=True)
            acc = _dot(v_block(own, hh), jnp.exp2(s - m_own).astype(_BF16))

            def visit_exact(j, carry, hh=hh):
                m_run, acc = carry
                t = _dot(k_block(j), q_ref[:, head_cols(hh)]) + bias_ref[pl.ds(j, 1), head_cols(hh)]
                m_new = jnp.maximum(m_run, jnp.max(t, axis=0, keepdims=True))
                acc = jnp.exp2(m_run - m_new) * acc + _dot(v_block(j, hh), jnp.exp2(t - m_new).astype(_BF16))
                return m_new, acc

            _, acc = lax.fori_loop(0, own, visit_exact, (m_own, acc))
            acc_ref[hh] = acc

    o_t = jnp.concatenate([acc_ref[hh, :head_dim] / acc_ref[hh, head_dim:head_dim + 1]
                           for hh in range(HEADS_PER_STEP)], axis=0)
    o_ref[0] = o_t.T.astype(_BF16)


def _moba(qt, k, vt, kmean):
    bsz, s, d = k.shape
    head_dim = d // N_HEADS
    nb = s // MOBA_BLOCK
    width = HEADS_PER_STEP * head_dim
    v_rows = head_dim + ONES_ROWS
    pos = jnp.arange(MOBA_BLOCK)
    causal = jnp.tile(jnp.where(pos[:, None] <= pos[None, :], 0.0, NEG_INF).astype(_F32), (1, HEADS_PER_STEP))
    return pl.pallas_call(
        functools.partial(_moba_kernel, head_dim=head_dim),
        grid=(bsz, d // width, nb),
        in_specs=[pl.BlockSpec((1, 1, width, MOBA_BLOCK), lambda b, c, i: (b, i, c, 0)),
                  pl.BlockSpec((1, s, width), lambda b, c, i: (b, 0, c)),
                  pl.BlockSpec((1, nb, HEADS_PER_STEP * v_rows, MOBA_BLOCK), lambda b, c, i: (b, 0, c, 0)),
                  pl.BlockSpec((1, nb, width), lambda b, c, i: (b, 0, c)),
                  pl.BlockSpec(causal.shape, lambda b, c, i: (0, 0))],
        out_specs=pl.BlockSpec((1, MOBA_BLOCK, width), lambda b, c, i: (b, i, c)),
        out_shape=jax.ShapeDtypeStruct((bsz, s, d), _BF16),
        scratch_shapes=[pltpu.VMEM((width, HEADS_PER_STEP * MOBA_BLOCK), _BF16),
                        pltpu.VMEM((nb, HEADS_PER_STEP * MOBA_BLOCK), _F32),
                        pltpu.VMEM((HEADS_PER_STEP, v_rows, MOBA_BLOCK), _F32)],
        compiler_params=pltpu.CompilerParams(
            dimension_semantics=("arbitrary", "arbitrary", "arbitrary"), vmem_limit_bytes=VMEM_LIMIT),
        name="moba_attention",
    )(qt, k, vt, kmean.reshape(bsz, nb, d), causal)


def kernel(x, mix_norm, sc_w_in, sc_w_conv, sc_w_out, moba_w_qkv, moba_w_o, ffn_norm, ffn_w_up, ffn_w_conv,
           ffn_w_down, final_norm):
    bsz, s, d = x.shape
    assert N_HEADS % HEADS_PER_STEP == 0 and (d // N_HEADS * HEADS_PER_STEP) % LANES == 0
    assert s % (MOBA_BLOCK * BLOCKS_PER_VISIT) == 0 and s % min(TM_FFN, s) == 0 and s % min(TM_MIXER, s) == 0
    assert mix_norm.shape[0] == 2 and ffn_w_down.shape[1] % CK_FFN == 0 and d % CK_MIXER == 0

    x = _mixer(x, mix_norm[0], sc_w_in[0], sc_w_conv[0], sc_w_out[0])
    x = _ffn(x, ffn_norm[0], ffn_w_up[0], ffn_w_conv[0], ffn_w_down[0])
    qt, k, vt, kmean = _qkv(x, mix_norm[1], moba_w_qkv[0], _rope_tables(s, d // N_HEADS))
    attn = _moba(qt, k, vt, kmean)
    return _ffn(x, ffn_norm[1], ffn_w_up[1], ffn_w_conv[1], ffn_w_down[1],
                attn=attn, w_o=moba_w_o[0], final_g=final_norm)
```

```python
import functools

import jax
import jax.numpy as jnp
from jax import lax
from jax.experimental import pallas as pl
from jax.experimental.pallas import tpu as pltpu

N_HEADS = 16
MOBA_BLOCK = 256
MOBA_TOPK = 3
ROPE_THETA = 10000.0
RMS_EPS = 1e-6
NEG_INF = -1e30
BELOW_NEG_INF = -3.0e38
LOG2E = 1.4426950408889634
CARRY_ROWS = 8
LANES = 128
MXU_COLS = 256
HEADS_PER_STEP = 4
ONES_ROWS = 16
GROUP_SIZES = (16, 8, 4)
EXPONENT_GUARD = 64.0

QKV_BLOCKS = 4
TM_MIXER = 1024
SUB_MIXER = 512
CK_MIXER = 512
TM_FFN = 1024
SUB_FFN = 256
CK_FFN = 256
VMEM_LIMIT = 56 * 1024 * 1024

_BF16 = jnp.bfloat16
_F32 = jnp.float32


def _dot(a, b):
    return jnp.dot(a, b, preferred_element_type=_F32)


def _rmsnorm(x, g):
    var = jnp.mean(x * x, axis=-1, keepdims=True)
    return x * lax.rsqrt(var + RMS_EPS) * g


def _delay_rows(u, prev, n):
    rolled = pltpu.roll(u, n, axis=0)
    tail = pltpu.roll(prev, n, axis=0)
    row = lax.broadcasted_iota(jnp.int32, prev.shape, 0)
    first = jnp.where(row < n, tail, rolled[:CARRY_ROWS])
    return jnp.concatenate([first, rolled[CARRY_ROWS:]], axis=0)


def _causal_conv3(u, prev, w):
    return _delay_rows(u, prev, 2) * w[0:1] + _delay_rows(u, prev, 1) * w[1:2] + u * w[2:3]


def _reset_carry_at_sequence_start(carry_ref):
    @pl.when(pl.program_id(1) == 0)
    def _():
        carry_ref[...] = jnp.zeros_like(carry_ref)


def _mixer_kernel(x_ref, g_ref, win_ref, wconv_ref, wout_ref, o_ref, carry_ref, *, ck):
    _reset_carry_at_sequence_start(carry_ref)
    d = x_ref.shape[-1]
    for r0 in range(0, x_ref.shape[1], SUB_MIXER):
        rows = slice(r0, r0 + SUB_MIXER)
        x = x_ref[0, rows]
        h = _rmsnorm(x, g_ref[...]).astype(_BF16)
        acc = x
        for lo in range(0, d, ck):
            bb = _dot(h, win_ref[:, lo:lo + ck])
            cv = _dot(h, win_ref[:, d + lo:d + lo + ck]) * _dot(h, win_ref[:, 2 * d + lo:2 * d + lo + ck])
            y = _causal_conv3(cv, carry_ref[:, lo:lo + ck], wconv_ref[:, lo:lo + ck])
            carry_ref[:, lo:lo + ck] = cv[-CARRY_ROWS:, :]
            acc = acc + _dot((bb * y).astype(_BF16), wout_ref[lo:lo + ck, :])
        o_ref[0, rows] = acc


def _const_spec(shape):
    zeros = (0,) * len(shape)
    return pl.BlockSpec(shape, lambda *_: zeros, pipeline_mode=pl.Buffered(1))


def _mixer(x, g, w_in, w_conv, w_out):
    bsz, s, d = x.shape
    tm = min(TM_MIXER, s)
    tile = pl.BlockSpec((1, tm, d), lambda b, t: (b, t, 0))
    return pl.pallas_call(
        functools.partial(_mixer_kernel, ck=CK_MIXER),
        grid=(bsz, s // tm),
        in_specs=[tile, _const_spec((1, d)), _const_spec((d, 3 * d)), _const_spec((3, d)), _const_spec((d, d))],
        out_specs=tile,
        out_shape=jax.ShapeDtypeStruct(x.shape, _F32),
        scratch_shapes=[pltpu.VMEM((CARRY_ROWS, d), _F32)],
        compiler_params=pltpu.CompilerParams(
            dimension_semantics=("arbitrary", "arbitrary"), vmem_limit_bytes=VMEM_LIMIT),
        name="mixer",
    )(x, g.reshape(1, d), w_in.astype(_BF16), w_conv, w_out.astype(_BF16))


def _ffn_body(x, g_ref, wup_ref, wconv_ref, wdown_ref, carry_ref, *, d_ff, ck):
    h = _rmsnorm(x, g_ref[...]).astype(_BF16)

    def up_project(lo):
        halves = []
        for col in (lo, d_ff + lo):
            up = _dot(h, wup_ref[:, col:col + ck])
            halves.append((up, carry_ref[:, col:col + ck]))
            carry_ref[:, col:col + ck] = up[-CARRY_ROWS:, :]
        return halves

    chunks = list(range(0, d_ff, ck))
    acc = x
    ahead = up_project(chunks[0])
    for c, lo in enumerate(chunks):
        gate, lin = (_causal_conv3(up, prev, wconv_ref[:, col:col + ck])
                     for (up, prev), col in zip(ahead, (lo, d_ff + lo)))
        act = (gate / (1.0 + jnp.exp(-gate)) * lin).astype(_BF16)
        if c + 1 < len(chunks):
            ahead = up_project(chunks[c + 1])
        acc = acc + _dot(act, wdown_ref[lo:lo + ck, :])
    return acc


def _ffn_kernel(x_ref, g_ref, wup_ref, wconv_ref, wdown_ref, o_ref, carry_ref, *, d_ff, ck):
    _reset_carry_at_sequence_start(carry_ref)
    for lo in range(0, x_ref.shape[1], SUB_FFN):
        rows = slice(lo, lo + SUB_FFN)
        o_ref[0, rows] = _ffn_body(x_ref[0, rows], g_ref, wup_ref, wconv_ref, wdown_ref, carry_ref, d_ff=d_ff, ck=ck)


def _proj_ffn_norm_kernel(x_ref, a_ref, wo_ref, g_ref, wup_ref, wconv_ref, wdown_ref, fg_ref, o_ref, carry_ref,
                          *, d_ff, ck):
    _reset_carry_at_sequence_start(carry_ref)
    for lo in range(0, x_ref.shape[1], SUB_FFN):
        rows = slice(lo, lo + SUB_FFN)
        x = x_ref[0, rows] + _dot(a_ref[0, rows], wo_ref[...])
        y = _ffn_body(x, g_ref, wup_ref, wconv_ref, wdown_ref, carry_ref, d_ff=d_ff, ck=ck)
        o_ref[0, rows] = _rmsnorm(y, fg_ref[...])


def _ffn(x, g, w_up, w_conv, w_down, attn=None, w_o=None, final_g=None):
    bsz, s, d = x.shape
    d_ff = w_down.shape[0]
    tm = min(TM_FFN, s)
    tile = pl.BlockSpec((1, tm, d), lambda b, t: (b, t, 0))
    ffn_specs = [_const_spec((1, d)), _const_spec((d, 2 * d_ff)), _const_spec((3, 2 * d_ff)), _const_spec((d_ff, d))]
    ffn_args = [g.reshape(1, d), w_up.astype(_BF16), w_conv, w_down.astype(_BF16)]
    if attn is None:
        body, name = _ffn_kernel, "ffn"
        in_specs, args = [tile] + ffn_specs, [x] + ffn_args
    else:
        body, name = _proj_ffn_norm_kernel, "proj_ffn_norm"
        in_specs = [tile, tile, _const_spec((d, d))] + ffn_specs + [_const_spec((1, d))]
        args = [x, attn, w_o.astype(_BF16)] + ffn_args + [final_g.reshape(1, d)]
    return pl.pallas_call(
        functools.partial(body, d_ff=d_ff, ck=CK_FFN),
        grid=(bsz, s // tm),
        in_specs=in_specs,
        out_specs=tile,
        out_shape=jax.ShapeDtypeStruct(x.shape, _F32),
        scratch_shapes=[pltpu.VMEM((CARRY_ROWS, 2 * d_ff), _F32)],
        compiler_params=pltpu.CompilerParams(
            dimension_semantics=("arbitrary", "arbitrary"), vmem_limit_bytes=VMEM_LIMIT),
        name=name,
    )(*args)


def _rope_table_kernel(inv_row_ref, sign_row_ref, inv_col_ref, cos_k_ref, sin_k_ref, cos_t_ref, sin_t_ref):
    base = pl.program_id(0) * MOBA_BLOCK
    pos_rows = (base + lax.broadcasted_iota(jnp.int32, cos_k_ref.shape, 0)).astype(_F32)
    ang = pos_rows * inv_row_ref[...]
    cos_k_ref[...] = jnp.cos(ang)
    sin_k_ref[...] = jnp.sin(ang) * sign_row_ref[...]
    pos_cols = (base + lax.broadcasted_iota(jnp.int32, cos_t_ref.shape[1:], 1)).astype(_F32)
    ang_t = pos_cols * inv_col_ref[...]
    cos_t_ref[0] = jnp.cos(ang_t)
    sin_t_ref[0] = jnp.sin(ang_t)


def _rope_tables(s, head_dim):
    half = head_dim // 2
    nb = s // MOBA_BLOCK
    inv = ROPE_THETA ** (-jnp.arange(half, dtype=_F32) / half)
    inv_row = jnp.tile(inv, LANES // half).reshape(1, LANES)
    sign_row = jnp.tile(jnp.concatenate([-jnp.ones(half, _F32), jnp.ones(half, _F32)]),
                        LANES // head_dim).reshape(1, LANES)
    inv_col = jnp.broadcast_to(inv[:, None], (half, MOBA_BLOCK))
    return pl.pallas_call(
        _rope_table_kernel,
        grid=(nb,),
        in_specs=[pl.BlockSpec((1, LANES), lambda t: (0, 0)), pl.BlockSpec((1, LANES), lambda t: (0, 0)),
                  pl.BlockSpec((half, MOBA_BLOCK), lambda t: (0, 0))],
        out_specs=[pl.BlockSpec((MOBA_BLOCK, LANES), lambda t: (t, 0)),
                   pl.BlockSpec((MOBA_BLOCK, LANES), lambda t: (t, 0)),
                   pl.BlockSpec((1, half, MOBA_BLOCK), lambda t: (t, 0, 0)),
                   pl.BlockSpec((1, half, MOBA_BLOCK), lambda t: (t, 0, 0))],
        out_shape=[jax.ShapeDtypeStruct((s, LANES), _F32), jax.ShapeDtypeStruct((s, LANES), _F32),
                   jax.ShapeDtypeStruct((nb, half, MOBA_BLOCK), _F32),
                   jax.ShapeDtypeStruct((nb, half, MOBA_BLOCK), _F32)],
        name="rope_tables",
    )(inv_row, sign_row, inv_col)


def _qkv_kernel(x_ref, g_ref, wqt_ref, wk_ref, wvt_ref, cos_k_ref, sin_k_ref, cos_t_ref, sin_t_ref,
                qt_ref, k_ref, vt_ref, kmean_ref, *, head_dim):
    d = x_ref.shape[-1]
    half = head_dim // 2
    v_rows = head_dim + ONES_ROWS
    nt = (((1,), (1,)), ((), ()))
    q_scale = head_dim ** -0.5 * LOG2E
    lane = lax.broadcasted_iota(jnp.int32, (MOBA_BLOCK, LANES), 1)
    first_half = (lane % head_dim) < half

    for blk in range(qt_ref.shape[1]):
        rows = slice(blk * MOBA_BLOCK, (blk + 1) * MOBA_BLOCK)
        h = _rmsnorm(x_ref[0, rows], g_ref[...]).astype(_BF16)

        qt = lax.dot_general(wqt_ref[...], h, nt, preferred_element_type=_F32)
        cos_t, sin_t = cos_t_ref[blk], sin_t_ref[blk]
        for lo in range(0, d, head_dim):
            x1, x2 = qt[lo:lo + half], qt[lo + half:lo + head_dim]
            qt_ref[0, blk, lo:lo + half, :] = ((x1 * cos_t - x2 * sin_t) * q_scale).astype(_BF16)
            qt_ref[0, blk, lo + half:lo + head_dim, :] = ((x2 * cos_t + x1 * sin_t) * q_scale).astype(_BF16)

        vt = lax.dot_general(wvt_ref[...], h, nt, preferred_element_type=_F32).astype(_BF16)
        for hd in range(d // head_dim):
            vt_ref[0, blk, hd * v_rows:hd * v_rows + head_dim, :] = vt[hd * head_dim:(hd + 1) * head_dim]
            vt_ref[0, blk, hd * v_rows + head_dim:(hd + 1) * v_rows, :] = jnp.ones((ONES_ROWS, MOBA_BLOCK), _BF16)

        cos_k, sin_k = cos_k_ref[rows], sin_k_ref[rows]
        for lo in range(0, d, MXU_COLS):
            k_wide = _dot(h, wk_ref[:, lo:lo + MXU_COLS])
            for sub in range(lo, lo + MXU_COLS, LANES):
                kk = k_wide[:, sub - lo:sub - lo + LANES]
                partner = jnp.where(first_half, pltpu.roll(kk, LANES - half, axis=1), pltpu.roll(kk, half, axis=1))
                roped = kk * cos_k + partner * sin_k
                k_ref[0, rows, sub:sub + LANES] = roped.astype(_BF16)
                kmean_ref[0, blk, :, sub:sub + LANES] = jnp.mean(roped, axis=0, keepdims=True)


def _qkv(x, g, w_qkv, tables):
    bsz, s, d = x.shape
    head_dim = d // N_HEADS
    half = head_dim // 2
    nb = s // MOBA_BLOCK
    wq, wk, wv = jnp.split(w_qkv.astype(_BF16), 3, axis=1)
    cos_k, sin_k, cos_t, sin_t = tables
    nq = min(QKV_BLOCKS, nb)
    tile = pl.BlockSpec((1, nq * MOBA_BLOCK, d), lambda b, t: (b, t, 0))
    tposed = pl.BlockSpec((1, nq, d, MOBA_BLOCK), lambda b, t: (b, t, 0, 0))
    d_aug = N_HEADS * (head_dim + ONES_ROWS)
    v_tposed = pl.BlockSpec((1, nq, d_aug, MOBA_BLOCK), lambda b, t: (b, t, 0, 0))
    k_table = pl.BlockSpec((nq * MOBA_BLOCK, LANES), lambda b, t: (t, 0))
    t_table = pl.BlockSpec((nq, half, MOBA_BLOCK), lambda b, t: (t, 0, 0))
    return pl.pallas_call(
        functools.partial(_qkv_kernel, head_dim=head_dim),
        grid=(bsz, nb // nq),
        in_specs=[tile, _const_spec((1, d)), _const_spec((d, d)), _const_spec((d, d)), _const_spec((d, d)),
                  k_table, k_table, t_table, t_table],
        out_specs=[tposed, tile, v_tposed, pl.BlockSpec((1, nq, 1, d), lambda b, t: (b, t, 0, 0))],
        out_shape=[jax.ShapeDtypeStruct((bsz, nb, d, MOBA_BLOCK), _BF16),
                   jax.ShapeDtypeStruct((bsz, s, d), _BF16),
                   jax.ShapeDtypeStruct((bsz, nb, d_aug, MOBA_BLOCK), _BF16),
                   jax.ShapeDtypeStruct((bsz, nb, 1, d), _F32)],
        compiler_params=pltpu.CompilerParams(
            dimension_semantics=("arbitrary", "arbitrary"), vmem_limit_bytes=VMEM_LIMIT),
        name="qkv_rope",
    )(x, g.reshape(1, d), wq.T, wk, wv.T, cos_k, sin_k, cos_t, sin_t)


def _moba_kernel(qt_ref, k_ref, vt_ref, kmean_ref, causal_ref, o_ref, q_ref, bias_ref, acc_ref, *, head_dim):
    own = pl.program_id(2)
    nb = kmean_ref.shape[1]
    blk = MOBA_BLOCK
    width = HEADS_PER_STEP * blk
    v_rows = head_dim + ONES_ROWS

    def head_cols(hh):
        return slice(hh * blk, (hh + 1) * blk)

    def k_block(j):
        return k_ref[0, pl.ds(pl.multiple_of(j * blk, blk), blk), :]

    def v_block(j, hh):
        return vt_ref[0, j, hh * v_rows:(hh + 1) * v_rows, :]

    q_ref[...] = jnp.zeros_like(q_ref)
    for hh in range(HEADS_PER_STEP):
        feats = slice(hh * head_dim, (hh + 1) * head_dim)
        q_ref[feats, head_cols(hh)] = qt_ref[0, 0, feats, :]
    q_all = q_ref[...]

    block_id = lax.broadcasted_iota(jnp.int32, (nb, width), 0)
    gate = jnp.where(block_id < own, _dot(kmean_ref[0].astype(_BF16), q_all), NEG_INF)
    bias = jnp.full((nb, width), NEG_INF, _F32)
    for _ in range(MOBA_TOPK):
        best = jnp.max(gate, axis=0, keepdims=True)
        first = jnp.min(jnp.where(gate == best, block_id, nb), axis=0, keepdims=True)
        picked = block_id == first
        bias = jnp.where(picked, 0.0, bias)
        gate = jnp.where(picked, BELOW_NEG_INF, gate)
    bias_ref[...] = jnp.where(block_id < own, bias, NEG_INF)

    s_own = _dot(k_block(own), q_all) + causal_ref[...]
    own_peak = jnp.max(s_own.reshape(blk // 8, 8, width), axis=0)
    p_own = jnp.exp2(s_own).astype(_BF16)
    for hh in range(HEADS_PER_STEP):
        acc_ref[hh] = _dot(v_block(own, hh), p_own[:, head_cols(hh)])

    def visit_groups(first_block, group, n_groups, peak):
        def visit(i, peak):
            j0 = first_block + i * group
            keys = k_ref[0, pl.ds(pl.multiple_of(j0 * blk, blk), group * blk), :]
            s = _dot(keys, q_ref[...])
            updates = [None] * HEADS_PER_STEP
            for g in range(group):
                t = s[g * blk:(g + 1) * blk] + bias_ref[pl.ds(j0 + g, 1), :]
                peak = jnp.maximum(peak, jnp.max(t.reshape(blk // 8, 8, width), axis=0))
                p = jnp.exp2(t).astype(_BF16)
                for hh in range(HEADS_PER_STEP):
                    u = _dot(v_block(j0 + g, hh), p[:, head_cols(hh)])
                    updates[hh] = u if updates[hh] is None else updates[hh] + u
            for hh in range(HEADS_PER_STEP):
                acc_ref[hh] += updates[hh]
            return peak

        return lax.fori_loop(0, n_groups, visit, peak)

    peak, first, left = own_peak, 0, own
    for size in GROUP_SIZES[:-1]:
        n_groups = left // size
        peak = visit_groups(first, size, n_groups, peak)
        first, left = first + n_groups * size, left - n_groups * size
    peak = visit_groups(first, GROUP_SIZES[-1], (left + GROUP_SIZES[-1] - 1) // GROUP_SIZES[-1], peak)
    query_peak = jnp.max(peak, axis=0, keepdims=True)
    out_of_range = (jnp.max(query_peak) > EXPONENT_GUARD) | (jnp.min(query_peak) < -EXPONENT_GUARD)

    @pl.when(out_of_range)
    def _():
        for hh in range(HEADS_PER_STEP):
            s = _dot(k_block(own), q_ref[:, head_cols(hh)]) + causal_ref[:, head_cols(hh)]
            m_own = jnp.max(s, axis=0, keepdims=True)
            acc = _dot(v_block(own, hh), jnp.exp2(s - m_own).astype(_BF16))

            def visit_exact(j, carry, hh=hh):
                m_run, acc = carry
                t = _dot(k_block(j), q_ref[:, head_cols(hh)]) + bias_ref[pl.ds(j, 1), head_cols(hh)]
                m_new = jnp.maximum(m_run, jnp.max(t, axis=0, keepdims=True))
                acc = jnp.exp2(m_run - m_new) * acc + _dot(v_block(j, hh), jnp.exp2(t - m_new).astype(_BF16))
                return m_new, acc

            _, acc = lax.fori_loop(0, own, visit_exact, (m_own, acc))
            acc_ref[hh] = acc

    o_t = jnp.concatenate([acc_ref[hh, :head_dim] / acc_ref[hh, head_dim:head_dim + 1]
                           for hh in range(HEADS_PER_STEP)], axis=0)
    o_ref[0] = o_t.T.astype(_BF16)


def _moba(qt, k, vt, kmean):
    bsz, s, d = k.shape
    head_dim = d // N_HEADS
    nb = s // MOBA_BLOCK
    width = HEADS_PER_STEP * head_dim
    v_rows = head_dim + ONES_ROWS
    pos = jnp.arange(MOBA_BLOCK)
    causal = jnp.tile(jnp.where(pos[:, None] <= pos[None, :], 0.0, NEG_INF).astype(_F32), (1, HEADS_PER_STEP))
    return pl.pallas_call(
        functools.partial(_moba_kernel, head_dim=head_dim),
        grid=(bsz, d // width, nb),
        in_specs=[pl.BlockSpec((1, 1, width, MOBA_BLOCK), lambda b, c, i: (b, i, c, 0)),
                  pl.BlockSpec((1, s, width), lambda b, c, i: (b, 0, c)),
                  pl.BlockSpec((1, nb, HEADS_PER_STEP * v_rows, MOBA_BLOCK), lambda b, c, i: (b, 0, c, 0)),
                  pl.BlockSpec((1, nb, width), lambda b, c, i: (b, 0, c)),
                  pl.BlockSpec(causal.shape, lambda b, c, i: (0, 0))],
        out_specs=pl.BlockSpec((1, MOBA_BLOCK, width), lambda b, c, i: (b, i, c)),
        out_shape=jax.ShapeDtypeStruct((bsz, s, d), _BF16),
        scratch_shapes=[pltpu.VMEM((width, HEADS_PER_STEP * MOBA_BLOCK), _BF16),
                        pltpu.VMEM((nb, HEADS_PER_STEP * MOBA_BLOCK), _F32),
                        pltpu.VMEM((HEADS_PER_STEP, v_rows, MOBA_BLOCK), _F32)],
        compiler_params=pltpu.CompilerParams(
            dimension_semantics=("arbitrary", "arbitrary", "arbitrary"), vmem_limit_bytes=VMEM_LIMIT),
        name="moba_attention",
    )(qt, k, vt, kmean.reshape(bsz, nb, d), causal)


def kernel(x, mix_norm, sc_w_in, sc_w_conv, sc_w_out, moba_w_qkv, moba_w_o, ffn_norm, ffn_w_up, ffn_w_conv,
           ffn_w_down, final_norm):
    bsz, s, d = x.shape
    assert N_HEADS % HEADS_PER_STEP == 0 and (d // N_HEADS * HEADS_PER_STEP) % LANES == 0
    assert s % (MOBA_BLOCK * GROUP_SIZES[-1]) == 0 and s % min(TM_FFN, s) == 0 and s % min(TM_MIXER, s) == 0
    assert mix_norm.shape[0] == 2 and ffn_w_down.shape[1] % CK_FFN == 0 and d % CK_MIXER == 0

    x = _mixer(x, mix_norm[0], sc_w_in[0], sc_w_conv[0], sc_w_out[0])
    x = _ffn(x, ffn_norm[0], ffn_w_up[0], ffn_w_conv[0], ffn_w_down[0])
    qt, k, vt, kmean = _qkv(x, mix_norm[1], moba_w_qkv[0], _rope_tables(s, d // N_HEADS))
    attn = _moba(qt, k, vt, kmean)
    return _ffn(x, ffn_norm[1], ffn_w_up[1], ffn_w_conv[1], ffn_w_down[1],
                attn=attn, w_o=moba_w_o[0], final_g=final_norm)
```

```python
import functools

import jax
import jax.numpy as jnp
from jax import lax
from jax.experimental import pallas as pl
from jax.experimental.pallas import tpu as pltpu

N_HEADS = 16
MOBA_BLOCK = 256
MOBA_TOPK = 3
ROPE_THETA = 10000.0
RMS_EPS = 1e-6
NEG_INF = -1e30
BELOW_NEG_INF = -3.0e38
LOG2E = 1.4426950408889634
CARRY_ROWS = 8
LANES = 128
MXU_COLS = 256
HEADS_PER_STEP = 4
ONES_ROWS = 16
GROUP_SIZES = (16, 8, 4, 2, 1)
EXPONENT_GUARD = 64.0

QKV_BLOCKS = 4
TM_MIXER = 1024
SUB_MIXER = 512
CK_MIXER = 512
TM_FFN = 1024
SUB_FFN = 256
CK_FFN = 256
VMEM_LIMIT = 56 * 1024 * 1024

_BF16 = jnp.bfloat16
_F32 = jnp.float32


def _dot(a, b):
    return jnp.dot(a, b, preferred_element_type=_F32)


def _rmsnorm(x, g):
    var = jnp.mean(x * x, axis=-1, keepdims=True)
    return x * lax.rsqrt(var + RMS_EPS) * g


def _delay_rows(u, prev, n):
    rolled = pltpu.roll(u, n, axis=0)
    tail = pltpu.roll(prev, n, axis=0)
    row = lax.broadcasted_iota(jnp.int32, prev.shape, 0)
    first = jnp.where(row < n, tail, rolled[:CARRY_ROWS])
    return jnp.concatenate([first, rolled[CARRY_ROWS:]], axis=0)


def _causal_conv3(u, prev, w):
    return _delay_rows(u, prev, 2) * w[0:1] + _delay_rows(u, prev, 1) * w[1:2] + u * w[2:3]


def _reset_carry_at_sequence_start(carry_ref):
    @pl.when(pl.program_id(1) == 0)
    def _():
        carry_ref[...] = jnp.zeros_like(carry_ref)


def _mixer_kernel(x_ref, g_ref, win_ref, wconv_ref, wout_ref, o_ref, carry_ref, *, ck):
    _reset_carry_at_sequence_start(carry_ref)
    d = x_ref.shape[-1]
    for r0 in range(0, x_ref.shape[1], SUB_MIXER):
        rows = slice(r0, r0 + SUB_MIXER)
        x = x_ref[0, rows]
        h = _rmsnorm(x, g_ref[...]).astype(_BF16)
        acc = x
        for lo in range(0, d, ck):
            bb = _dot(h, win_ref[:, lo:lo + ck])
            cv = _dot(h, win_ref[:, d + lo:d + lo + ck]) * _dot(h, win_ref[:, 2 * d + lo:2 * d + lo + ck])
            y = _causal_conv3(cv, carry_ref[:, lo:lo + ck], wconv_ref[:, lo:lo + ck])
            carry_ref[:, lo:lo + ck] = cv[-CARRY_ROWS:, :]
            acc = acc + _dot((bb * y).astype(_BF16), wout_ref[lo:lo + ck, :])
        o_ref[0, rows] = acc


def _const_spec(shape):
    zeros = (0,) * len(shape)
    return pl.BlockSpec(shape, lambda *_: zeros, pipeline_mode=pl.Buffered(1))


def _mixer(x, g, w_in, w_conv, w_out):
    bsz, s, d = x.shape
    tm = min(TM_MIXER, s)
    tile = pl.BlockSpec((1, tm, d), lambda b, t: (b, t, 0))
    return pl.pallas_call(
        functools.partial(_mixer_kernel, ck=CK_MIXER),
        grid=(bsz, s // tm),
        in_specs=[tile, _const_spec((1, d)), _const_spec((d, 3 * d)), _const_spec((3, d)), _const_spec((d, d))],
        out_specs=tile,
        out_shape=jax.ShapeDtypeStruct(x.shape, _F32),
        scratch_shapes=[pltpu.VMEM((CARRY_ROWS, d), _F32)],
        compiler_params=pltpu.CompilerParams(
            dimension_semantics=("arbitrary", "arbitrary"), vmem_limit_bytes=VMEM_LIMIT),
        name="mixer",
    )(x, g.reshape(1, d), w_in.astype(_BF16), w_conv, w_out.astype(_BF16))


def _ffn_body(x, g_ref, wup_ref, wconv_ref, wdown_ref, carry_ref, *, d_ff, ck):
    h = _rmsnorm(x, g_ref[...]).astype(_BF16)

    def up_project(lo):
        halves = []
        for col in (lo, d_ff + lo):
            up = _dot(h, wup_ref[:, col:col + ck])
            halves.append((up, carry_ref[:, col:col + ck]))
            carry_ref[:, col:col + ck] = up[-CARRY_ROWS:, :]
        return halves

    chunks = list(range(0, d_ff, ck))
    acc = x
    ahead = up_project(chunks[0])
    for c, lo in enumerate(chunks):
        gate, lin = (_causal_conv3(up, prev, wconv_ref[:, col:col + ck])
                     for (up, prev), col in zip(ahead, (lo, d_ff + lo)))
        act = (gate / (1.0 + jnp.exp(-gate)) * lin).astype(_BF16)
        if c + 1 < len(chunks):
            ahead = up_project(chunks[c + 1])
        acc = acc + _dot(act, wdown_ref[lo:lo + ck, :])
    return acc


def _ffn_kernel(x_ref, g_ref, wup_ref, wconv_ref, wdown_ref, o_ref, carry_ref, *, d_ff, ck):
    _reset_carry_at_sequence_start(carry_ref)
    for lo in range(0, x_ref.shape[1], SUB_FFN):
        rows = slice(lo, lo + SUB_FFN)
        o_ref[0, rows] = _ffn_body(x_ref[0, rows], g_ref, wup_ref, wconv_ref, wdown_ref, carry_ref, d_ff=d_ff, ck=ck)


def _proj_ffn_norm_kernel(x_ref, a_ref, wo_ref, g_ref, wup_ref, wconv_ref, wdown_ref, fg_ref, o_ref, carry_ref,
                          *, d_ff, ck):
    _reset_carry_at_sequence_start(carry_ref)
    for lo in range(0, x_ref.shape[1], SUB_FFN):
        rows = slice(lo, lo + SUB_FFN)
        x = x_ref[0, rows] + _dot(a_ref[0, rows], wo_ref[...])
        y = _ffn_body(x, g_ref, wup_ref, wconv_ref, wdown_ref, carry_ref, d_ff=d_ff, ck=ck)
        o_ref[0, rows] = _rmsnorm(y, fg_ref[...])


def _ffn(x, g, w_up, w_conv, w_down, attn=None, w_o=None, final_g=None):
    bsz, s, d = x.shape
    d_ff = w_down.shape[0]
    tm = min(TM_FFN, s)
    tile = pl.BlockSpec((1, tm, d), lambda b, t: (b, t, 0))
    ffn_specs = [_const_spec((1, d)), _const_spec((d, 2 * d_ff)), _const_spec((3, 2 * d_ff)), _const_spec((d_ff, d))]
    ffn_args = [g.reshape(1, d), w_up.astype(_BF16), w_conv, w_down.astype(_BF16)]
    if attn is None:
        body, name = _ffn_kernel, "ffn"
        in_specs, args = [tile] + ffn_specs, [x] + ffn_args
    else:
        body, name = _proj_ffn_norm_kernel, "proj_ffn_norm"
        in_specs = [tile, tile, _const_spec((d, d))] + ffn_specs + [_const_spec((1, d))]
        args = [x, attn, w_o.astype(_BF16)] + ffn_args + [final_g.reshape(1, d)]
    return pl.pallas_call(
        functools.partial(body, d_ff=d_ff, ck=CK_FFN),
        grid=(bsz, s // tm),
        in_specs=in_specs,
        out_specs=tile,
        out_shape=jax.ShapeDtypeStruct(x.shape, _F32),
        scratch_shapes=[pltpu.VMEM((CARRY_ROWS, 2 * d_ff), _F32)],
        compiler_params=pltpu.CompilerParams(
            dimension_semantics=("arbitrary", "arbitrary"), vmem_limit_bytes=VMEM_LIMIT),
        name=name,
    )(*args)


def _rope_table_kernel(inv_row_ref, sign_row_ref, inv_col_ref, cos_k_ref, sin_k_ref, cos_t_ref, sin_t_ref):
    base = pl.program_id(0) * MOBA_BLOCK
    pos_rows = (base + lax.broadcasted_iota(jnp.int32, cos_k_ref.shape, 0)).astype(_F32)
    ang = pos_rows * inv_row_ref[...]
    cos_k_ref[...] = jnp.cos(ang)
    sin_k_ref[...] = jnp.sin(ang) * sign_row_ref[...]
    pos_cols = (base + lax.broadcasted_iota(jnp.int32, cos_t_ref.shape[1:], 1)).astype(_F32)
    ang_t = pos_cols * inv_col_ref[...]
    cos_t_ref[0] = jnp.cos(ang_t)
    sin_t_ref[0] = jnp.sin(ang_t)


def _rope_tables(s, head_dim):
    half = head_dim // 2
    nb = s // MOBA_BLOCK
    inv = ROPE_THETA ** (-jnp.arange(half, dtype=_F32) / half)
    inv_row = jnp.tile(inv, LANES // half).reshape(1, LANES)
    sign_row = jnp.tile(jnp.concatenate([-jnp.ones(half, _F32), jnp.ones(half, _F32)]),
                        LANES // head_dim).reshape(1, LANES)
    inv_col = jnp.broadcast_to(inv[:, None], (half, MOBA_BLOCK))
    return pl.pallas_call(
        _rope_table_kernel,
        grid=(nb,),
        in_specs=[pl.BlockSpec((1, LANES), lambda t: (0, 0)), pl.BlockSpec((1, LANES), lambda t: (0, 0)),
                  pl.BlockSpec((half, MOBA_BLOCK), lambda t: (0, 0))],
        out_specs=[pl.BlockSpec((MOBA_BLOCK, LANES), lambda t: (t, 0)),
                   pl.BlockSpec((MOBA_BLOCK, LANES), lambda t: (t, 0)),
                   pl.BlockSpec((1, half, MOBA_BLOCK), lambda t: (t, 0, 0)),
                   pl.BlockSpec((1, half, MOBA_BLOCK), lambda t: (t, 0, 0))],
        out_shape=[jax.ShapeDtypeStruct((s, LANES), _F32), jax.ShapeDtypeStruct((s, LANES), _F32),
                   jax.ShapeDtypeStruct((nb, half, MOBA_BLOCK), _F32),
                   jax.ShapeDtypeStruct((nb, half, MOBA_BLOCK), _F32)],
        name="rope_tables",
    )(inv_row, sign_row, inv_col)


def _qkv_kernel(x_ref, g_ref, wqt_ref, wk_ref, wvt_ref, cos_k_ref, sin_k_ref, cos_t_ref, sin_t_ref,
                qt_ref, k_ref, vt_ref, kmean_ref, *, head_dim):
    d = x_ref.shape[-1]
    half = head_dim // 2
    v_rows = head_dim + ONES_ROWS
    nt = (((1,), (1,)), ((), ()))
    q_scale = head_dim ** -0.5 * LOG2E
    lane = lax.broadcasted_iota(jnp.int32, (MOBA_BLOCK, LANES), 1)
    first_half = (lane % head_dim) < half

    for blk in range(qt_ref.shape[1]):
        rows = slice(blk * MOBA_BLOCK, (blk + 1) * MOBA_BLOCK)
        h = _rmsnorm(x_ref[0, rows], g_ref[...]).astype(_BF16)

        qt = lax.dot_general(wqt_ref[...], h, nt, preferred_element_type=_F32)
        cos_t, sin_t = cos_t_ref[blk], sin_t_ref[blk]
        for lo in range(0, d, head_dim):
            x1, x2 = qt[lo:lo + half], qt[lo + half:lo + head_dim]
            qt_ref[0, blk, lo:lo + half, :] = ((x1 * cos_t - x2 * sin_t) * q_scale).astype(_BF16)
            qt_ref[0, blk, lo + half:lo + head_dim, :] = ((x2 * cos_t + x1 * sin_t) * q_scale).astype(_BF16)

        vt = lax.dot_general(wvt_ref[...], h, nt, preferred_element_type=_F32).astype(_BF16)
        for hd in range(d // head_dim):
            vt_ref[0, blk, hd * v_rows:hd * v_rows + head_dim, :] = vt[hd * head_dim:(hd + 1) * head_dim]
            vt_ref[0, blk, hd * v_rows + head_dim:(hd + 1) * v_rows, :] = jnp.ones((ONES_ROWS, MOBA_BLOCK), _BF16)

        cos_k, sin_k = cos_k_ref[rows], sin_k_ref[rows]
        for lo in range(0, d, MXU_COLS):
            k_wide = _dot(h, wk_ref[:, lo:lo + MXU_COLS])
            for sub in range(lo, lo + MXU_COLS, LANES):
                kk = k_wide[:, sub - lo:sub - lo + LANES]
                partner = jnp.where(first_half, pltpu.roll(kk, LANES - half, axis=1), pltpu.roll(kk, half, axis=1))
                roped = kk * cos_k + partner * sin_k
                k_ref[0, rows, sub:sub + LANES] = roped.astype(_BF16)
                kmean_ref[0, blk, :, sub:sub + LANES] = jnp.mean(roped, axis=0, keepdims=True)


def _qkv(x, g, w_qkv, tables):
    bsz, s, d = x.shape
    head_dim = d // N_HEADS
    half = head_dim // 2
    nb = s // MOBA_BLOCK
    wq, wk, wv = jnp.split(w_qkv.astype(_BF16), 3, axis=1)
    cos_k, sin_k, cos_t, sin_t = tables
    nq = min(QKV_BLOCKS, nb)
    tile = pl.BlockSpec((1, nq * MOBA_BLOCK, d), lambda b, t: (b, t, 0))
    tposed = pl.BlockSpec((1, nq, d, MOBA_BLOCK), lambda b, t: (b, t, 0, 0))
    d_aug = N_HEADS * (head_dim + ONES_ROWS)
    v_tposed = pl.BlockSpec((1, nq, d_aug, MOBA_BLOCK), lambda b, t: (b, t, 0, 0))
    k_table = pl.BlockSpec((nq * MOBA_BLOCK, LANES), lambda b, t: (t, 0))
    t_table = pl.BlockSpec((nq, half, MOBA_BLOCK), lambda b, t: (t, 0, 0))
    return pl.pallas_call(
        functools.partial(_qkv_kernel, head_dim=head_dim),
        grid=(bsz, nb // nq),
        in_specs=[tile, _const_spec((1, d)), _const_spec((d, d)), _const_spec((d, d)), _const_spec((d, d)),
                  k_table, k_table, t_table, t_table],
        out_specs=[tposed, tile, v_tposed, pl.BlockSpec((1, nq, 1, d), lambda b, t: (b, t, 0, 0))],
        out_shape=[jax.ShapeDtypeStruct((bsz, nb, d, MOBA_BLOCK), _BF16),
                   jax.ShapeDtypeStruct((bsz, s, d), _BF16),
                   jax.ShapeDtypeStruct((bsz, nb, d_aug, MOBA_BLOCK), _BF16),
                   jax.ShapeDtypeStruct((bsz, nb, 1, d), _F32)],
        compiler_params=pltpu.CompilerParams(
            dimension_semantics=("arbitrary", "arbitrary"), vmem_limit_bytes=VMEM_LIMIT),
        name="qkv_rope",
    )(x, g.reshape(1, d), wq.T, wk, wv.T, cos_k, sin_k, cos_t, sin_t)


def _moba_kernel(qt_ref, k_ref, vt_ref, kmean_ref, causal_ref, o_ref, q_ref, bias_ref, acc_ref, *, head_dim):
    own = pl.program_id(2)
    nb = kmean_ref.shape[1]
    blk = MOBA_BLOCK
    width = HEADS_PER_STEP * blk
    v_rows = head_dim + ONES_ROWS

    def head_cols(hh):
        return slice(hh * blk, (hh + 1) * blk)

    def k_block(j):
        return k_ref[0, pl.ds(pl.multiple_of(j * blk, blk), blk), :]

    def v_block(j, hh):
        return vt_ref[0, j, hh * v_rows:(hh + 1) * v_rows, :]

    q_ref[...] = jnp.zeros_like(q_ref)
    for hh in range(HEADS_PER_STEP):
        feats = slice(hh * head_dim, (hh + 1) * head_dim)
        q_ref[feats, head_cols(hh)] = qt_ref[0, 0, feats, :]
    q_all = q_ref[...]

    block_id = lax.broadcasted_iota(jnp.int32, (nb, width), 0)
    gate = jnp.where(block_id < own, _dot(kmean_ref[0].astype(_BF16), q_all), NEG_INF)
    bias = jnp.full((nb, width), NEG_INF, _F32)
    for _ in range(MOBA_TOPK):
        best = jnp.max(gate, axis=0, keepdims=True)
        first = jnp.min(jnp.where(gate == best, block_id, nb), axis=0, keepdims=True)
        picked = block_id == first
        bias = jnp.where(picked, 0.0, bias)
        gate = jnp.where(picked, BELOW_NEG_INF, gate)
    bias_ref[...] = jnp.where(block_id < own, bias, NEG_INF)

    s_own = _dot(k_block(own), q_all) + causal_ref[...]
    own_peak = jnp.max(s_own.reshape(blk // 8, 8, width), axis=0)
    p_own = jnp.exp2(s_own).astype(_BF16)
    for hh in range(HEADS_PER_STEP):
        acc_ref[hh] = _dot(v_block(own, hh), p_own[:, head_cols(hh)])

    def visit_groups(first_block, group, n_groups, peak):
        def visit(i, peak):
            j0 = first_block + i * group
            keys = k_ref[0, pl.ds(pl.multiple_of(j0 * blk, blk), group * blk), :]
            s = _dot(keys, q_ref[...])
            updates = [None] * HEADS_PER_STEP
            for g in range(group):
                t = s[g * blk:(g + 1) * blk] + bias_ref[pl.ds(j0 + g, 1), :]
                peak = jnp.maximum(peak, jnp.max(t.reshape(blk // 8, 8, width), axis=0))
                p = jnp.exp2(t).astype(_BF16)
                for hh in range(HEADS_PER_STEP):
                    u = _dot(v_block(j0 + g, hh), p[:, head_cols(hh)])
                    updates[hh] = u if updates[hh] is None else updates[hh] + u
            for hh in range(HEADS_PER_STEP):
                acc_ref[hh] += updates[hh]
            return peak

        return lax.fori_loop(0, n_groups, visit, peak)

    peak, first, left = own_peak, 0, own
    for size in GROUP_SIZES[:-1]:
        n_groups = left // size
        peak = visit_groups(first, size, n_groups, peak)
        first, left = first + n_groups * size, left - n_groups * size
    peak = visit_groups(first, GROUP_SIZES[-1], (left + GROUP_SIZES[-1] - 1) // GROUP_SIZES[-1], peak)
    query_peak = jnp.max(peak, axis=0, keepdims=True)
    out_of_range = (jnp.max(query_peak) > EXPONENT_GUARD) | (jnp.min(query_peak) < -EXPONENT_GUARD)

    @pl.when(out_of_range)
    def _():
        for hh in range(HEADS_PER_STEP):
            s = _dot(k_block(own), q_ref[:, head_cols(hh)]) + causal_ref[:, head_cols(hh)]
            m_own = jnp.max(s, axis=0, keepdims=True)
            acc = _dot(v_block(own, hh), jnp.exp2(s - m_own).astype(_BF16))

            def visit_exact(j, carry, hh=hh):
                m_run, acc = carry
                t = _dot(k_block(j), q_ref[:, head_cols(hh)]) + bias_ref[pl.ds(j, 1), head_cols(hh)]
                m_new = jnp.maximum(m_run, jnp.max(t, axis=0, keepdims=True))
                acc = jnp.exp2(m_run - m_new) * acc + _dot(v_block(j, hh), jnp.exp2(t - m_new).astype(_BF16))
                return m_new, acc

            _, acc = lax.fori_loop(0, own, visit_exact, (m_own, acc))
            acc_ref[hh] = acc

    o_t = jnp.concatenate([acc_ref[hh, :head_dim] / acc_ref[hh, head_dim:head_dim + 1]
                           for hh in range(HEADS_PER_STEP)], axis=0)
    o_ref[0] = o_t.T.astype(_BF16)


def _moba(qt, k, vt, kmean):
    bsz, s, d = k.shape
    head_dim = d // N_HEADS
    nb = s // MOBA_BLOCK
    width = HEADS_PER_STEP * head_dim
    v_rows = head_dim + ONES_ROWS
    pos = jnp.arange(MOBA_BLOCK)
    causal = jnp.tile(jnp.where(pos[:, None] <= pos[None, :], 0.0, NEG_INF).astype(_F32), (1, HEADS_PER_STEP))
    return pl.pallas_call(
        functools.partial(_moba_kernel, head_dim=head_dim),
        grid=(bsz, d // width, nb),
        in_specs=[pl.BlockSpec((1, 1, width, MOBA_BLOCK), lambda b, c, i: (b, i, c, 0)),
                  pl.BlockSpec((1, s, width), lambda b, c, i: (b, 0, c)),
                  pl.BlockSpec((1, nb, HEADS_PER_STEP * v_rows, MOBA_BLOCK), lambda b, c, i: (b, 0, c, 0)),
                  pl.BlockSpec((1, nb, width), lambda b, c, i: (b, 0, c)),
                  pl.BlockSpec(causal.shape, lambda b, c, i: (0, 0))],
        out_specs=pl.BlockSpec((1, MOBA_BLOCK, width), lambda b, c, i: (b, i, c)),
        out_shape=jax.ShapeDtypeStruct((bsz, s, d), _BF16),
        scratch_shapes=[pltpu.VMEM((width, HEADS_PER_STEP * MOBA_BLOCK), _BF16),
                        pltpu.VMEM((nb, HEADS_PER_STEP * MOBA_BLOCK), _F32),
                        pltpu.VMEM((HEADS_PER_STEP, v_rows, MOBA_BLOCK), _F32)],
        compiler_params=pltpu.CompilerParams(
            dimension_semantics=("arbitrary", "arbitrary", "arbitrary"), vmem_limit_bytes=VMEM_LIMIT),
        name="moba_attention",
    )(qt, k, vt, kmean.reshape(bsz, nb, d), causal)


def kernel(x, mix_norm, sc_w_in, sc_w_conv, sc_w_out, moba_w_qkv, moba_w_o, ffn_norm, ffn_w_up, ffn_w_conv,
           ffn_w_down, final_norm):
    bsz, s, d = x.shape
    assert N_HEADS % HEADS_PER_STEP == 0 and (d // N_HEADS * HEADS_PER_STEP) % LANES == 0
    assert s % (MOBA_BLOCK * GROUP_SIZES[-1]) == 0 and s % min(TM_FFN, s) == 0 and s % min(TM_MIXER, s) == 0
    assert mix_norm.shape[0] == 2 and ffn_w_down.shape[1] % CK_FFN == 0 and d % CK_MIXER == 0

    x = _mixer(x, mix_norm[0], sc_w_in[0], sc_w_conv[0], sc_w_out[0])
    x = _ffn(x, ffn_norm[0], ffn_w_up[0], ffn_w_conv[0], ffn_w_down[0])
    qt, k, vt, kmean = _qkv(x, mix_norm[1], moba_w_qkv[0], _rope_tables(s, d // N_HEADS))
    attn = _moba(qt, k, vt, kmean)
    return _ffn(x, ffn_norm[1], ffn_w_up[1], ffn_w_conv[1], ffn_w_down[1],
                attn=attn, w_o=moba_w_o[0], final_g=final_norm)
```

```python
import functools

import jax
import jax.numpy as jnp
from jax import lax
from jax.experimental import pallas as pl
from jax.experimental.pallas import tpu as pltpu

N_HEADS = 16
MOBA_BLOCK = 256
MOBA_TOPK = 3
ROPE_THETA = 10000.0
RMS_EPS = 1e-6
NEG_INF = -1e30
BELOW_NEG_INF = -3.0e38
LOG2E = 1.4426950408889634
CARRY_ROWS = 8
LANES = 128
MXU_COLS = 256
HEADS_PER_STEP = 4
ONES_ROWS = 16
GROUP_SIZES = (16, 8, 4, 2, 1)
EXPONENT_GUARD = 64.0

ATTN_TILES = 2
QKV_BLOCKS = 4
TM_MIXER = 1024
SUB_MIXER = 512
CK_MIXER = 512
TM_FFN = 1024
SUB_FFN = 256
CK_FFN = 256
VMEM_LIMIT = 56 * 1024 * 1024

_BF16 = jnp.bfloat16
_F32 = jnp.float32


def _dot(a, b):
    return jnp.dot(a, b, preferred_element_type=_F32)


def _rmsnorm(x, g):
    var = jnp.mean(x * x, axis=-1, keepdims=True)
    return x * lax.rsqrt(var + RMS_EPS) * g


def _delay_rows(u, prev, n):
    rolled = pltpu.roll(u, n, axis=0)
    tail = pltpu.roll(prev, n, axis=0)
    row = lax.broadcasted_iota(jnp.int32, prev.shape, 0)
    first = jnp.where(row < n, tail, rolled[:CARRY_ROWS])
    return jnp.concatenate([first, rolled[CARRY_ROWS:]], axis=0)


def _causal_conv3(u, prev, w):
    return _delay_rows(u, prev, 2) * w[0:1] + _delay_rows(u, prev, 1) * w[1:2] + u * w[2:3]


def _reset_carry_at_sequence_start(carry_ref):
    @pl.when(pl.program_id(1) == 0)
    def _():
        carry_ref[...] = jnp.zeros_like(carry_ref)


def _mixer_kernel(x_ref, g_ref, win_ref, wconv_ref, wout_ref, o_ref, carry_ref, *, ck):
    _reset_carry_at_sequence_start(carry_ref)
    d = x_ref.shape[-1]
    for r0 in range(0, x_ref.shape[1], SUB_MIXER):
        rows = slice(r0, r0 + SUB_MIXER)
        x = x_ref[0, rows]
        h = _rmsnorm(x, g_ref[...]).astype(_BF16)
        acc = x
        for lo in range(0, d, ck):
            bb = _dot(h, win_ref[:, lo:lo + ck])
            cv = _dot(h, win_ref[:, d + lo:d + lo + ck]) * _dot(h, win_ref[:, 2 * d + lo:2 * d + lo + ck])
            y = _causal_conv3(cv, carry_ref[:, lo:lo + ck], wconv_ref[:, lo:lo + ck])
            carry_ref[:, lo:lo + ck] = cv[-CARRY_ROWS:, :]
            acc = acc + _dot((bb * y).astype(_BF16), wout_ref[lo:lo + ck, :])
        o_ref[0, rows] = acc


def _const_spec(shape):
    zeros = (0,) * len(shape)
    return pl.BlockSpec(shape, lambda *_: zeros, pipeline_mode=pl.Buffered(1))


def _mixer(x, g, w_in, w_conv, w_out):
    bsz, s, d = x.shape
    tm = min(TM_MIXER, s)
    tile = pl.BlockSpec((1, tm, d), lambda b, t: (b, t, 0))
    return pl.pallas_call(
        functools.partial(_mixer_kernel, ck=CK_MIXER),
        grid=(bsz, s // tm),
        in_specs=[tile, _const_spec((1, d)), _const_spec((d, 3 * d)), _const_spec((3, d)), _const_spec((d, d))],
        out_specs=tile,
        out_shape=jax.ShapeDtypeStruct(x.shape, _F32),
        scratch_shapes=[pltpu.VMEM((CARRY_ROWS, d), _F32)],
        compiler_params=pltpu.CompilerParams(
            dimension_semantics=("arbitrary", "arbitrary"), vmem_limit_bytes=VMEM_LIMIT),
        name="mixer",
    )(x, g.reshape(1, d), w_in.astype(_BF16), w_conv, w_out.astype(_BF16))


def _ffn_body(x, g_ref, wup_ref, wconv_ref, wdown_ref, carry_ref, *, d_ff, ck):
    h = _rmsnorm(x, g_ref[...]).astype(_BF16)

    def up_project(lo):
        halves = []
        for col in (lo, d_ff + lo):
            up = _dot(h, wup_ref[:, col:col + ck])
            halves.append((up, carry_ref[:, col:col + ck]))
            carry_ref[:, col:col + ck] = up[-CARRY_ROWS:, :]
        return halves

    chunks = list(range(0, d_ff, ck))
    acc = x
    ahead = up_project(chunks[0])
    for c, lo in enumerate(chunks):
        gate, lin = (_causal_conv3(up, prev, wconv_ref[:, col:col + ck])
                     for (up, prev), col in zip(ahead, (lo, d_ff + lo)))
        act = (gate / (1.0 + jnp.exp(-gate)) * lin).astype(_BF16)
        if c + 1 < len(chunks):
            ahead = up_project(chunks[c + 1])
        acc = acc + _dot(act, wdown_ref[lo:lo + ck, :])
    return acc


def _ffn_kernel(x_ref, g_ref, wup_ref, wconv_ref, wdown_ref, o_ref, carry_ref, *, d_ff, ck):
    _reset_carry_at_sequence_start(carry_ref)
    for lo in range(0, x_ref.shape[1], SUB_FFN):
        rows = slice(lo, lo + SUB_FFN)
        o_ref[0, rows] = _ffn_body(x_ref[0, rows], g_ref, wup_ref, wconv_ref, wdown_ref, carry_ref, d_ff=d_ff, ck=ck)


def _proj_ffn_norm_kernel(x_ref, a_ref, wo_ref, g_ref, wup_ref, wconv_ref, wdown_ref, fg_ref, o_ref, carry_ref,
                          *, d_ff, ck):
    _reset_carry_at_sequence_start(carry_ref)
    for lo in range(0, x_ref.shape[1], SUB_FFN):
        rows = slice(lo, lo + SUB_FFN)
        x = x_ref[0, rows] + _dot(a_ref[0, rows], wo_ref[...])
        y = _ffn_body(x, g_ref, wup_ref, wconv_ref, wdown_ref, carry_ref, d_ff=d_ff, ck=ck)
        o_ref[0, rows] = _rmsnorm(y, fg_ref[...])


def _ffn(x, g, w_up, w_conv, w_down, attn=None, w_o=None, final_g=None):
    bsz, s, d = x.shape
    d_ff = w_down.shape[0]
    tm = min(TM_FFN, s)
    tile = pl.BlockSpec((1, tm, d), lambda b, t: (b, t, 0))
    ffn_specs = [_const_spec((1, d)), _const_spec((d, 2 * d_ff)), _const_spec((3, 2 * d_ff)), _const_spec((d_ff, d))]
    ffn_args = [g.reshape(1, d), w_up.astype(_BF16), w_conv, w_down.astype(_BF16)]
    if attn is None:
        body, name = _ffn_kernel, "ffn"
        in_specs, args = [tile] + ffn_specs, [x] + ffn_args
    else:
        body, name = _proj_ffn_norm_kernel, "proj_ffn_norm"
        in_specs = [tile, tile, _const_spec((d, d))] + ffn_specs + [_const_spec((1, d))]
        args = [x, attn, w_o.astype(_BF16)] + ffn_args + [final_g.reshape(1, d)]
    return pl.pallas_call(
        functools.partial(body, d_ff=d_ff, ck=CK_FFN),
        grid=(bsz, s // tm),
        in_specs=in_specs,
        out_specs=tile,
        out_shape=jax.ShapeDtypeStruct(x.shape, _F32),
        scratch_shapes=[pltpu.VMEM((CARRY_ROWS, 2 * d_ff), _F32)],
        compiler_params=pltpu.CompilerParams(
            dimension_semantics=("arbitrary", "arbitrary"), vmem_limit_bytes=VMEM_LIMIT),
        name=name,
    )(*args)


def _rope_table_kernel(inv_row_ref, sign_row_ref, inv_col_ref, cos_k_ref, sin_k_ref, cos_t_ref, sin_t_ref):
    base = pl.program_id(0) * MOBA_BLOCK
    pos_rows = (base + lax.broadcasted_iota(jnp.int32, cos_k_ref.shape, 0)).astype(_F32)
    ang = pos_rows * inv_row_ref[...]
    cos_k_ref[...] = jnp.cos(ang)
    sin_k_ref[...] = jnp.sin(ang) * sign_row_ref[...]
    pos_cols = (base + lax.broadcasted_iota(jnp.int32, cos_t_ref.shape[1:], 1)).astype(_F32)
    ang_t = pos_cols * inv_col_ref[...]
    cos_t_ref[0] = jnp.cos(ang_t)
    sin_t_ref[0] = jnp.sin(ang_t)


def _rope_tables(s, head_dim):
    half = head_dim // 2
    nb = s // MOBA_BLOCK
    inv = ROPE_THETA ** (-jnp.arange(half, dtype=_F32) / half)
    inv_row = jnp.tile(inv, LANES // half).reshape(1, LANES)
    sign_row = jnp.tile(jnp.concatenate([-jnp.ones(half, _F32), jnp.ones(half, _F32)]),
                        LANES // head_dim).reshape(1, LANES)
    inv_col = jnp.broadcast_to(inv[:, None], (half, MOBA_BLOCK))
    return pl.pallas_call(
        _rope_table_kernel,
        grid=(nb,),
        in_specs=[pl.BlockSpec((1, LANES), lambda t: (0, 0)), pl.BlockSpec((1, LANES), lambda t: (0, 0)),
                  pl.BlockSpec((half, MOBA_BLOCK), lambda t: (0, 0))],
        out_specs=[pl.BlockSpec((MOBA_BLOCK, LANES), lambda t: (t, 0)),
                   pl.BlockSpec((MOBA_BLOCK, LANES), lambda t: (t, 0)),
                   pl.BlockSpec((1, half, MOBA_BLOCK), lambda t: (t, 0, 0)),
                   pl.BlockSpec((1, half, MOBA_BLOCK), lambda t: (t, 0, 0))],
        out_shape=[jax.ShapeDtypeStruct((s, LANES), _F32), jax.ShapeDtypeStruct((s, LANES), _F32),
                   jax.ShapeDtypeStruct((nb, half, MOBA_BLOCK), _F32),
                   jax.ShapeDtypeStruct((nb, half, MOBA_BLOCK), _F32)],
        name="rope_tables",
    )(inv_row, sign_row, inv_col)


def _qkv_kernel(x_ref, g_ref, wqt_ref, wk_ref, wvt_ref, cos_k_ref, sin_k_ref, cos_t_ref, sin_t_ref,
                qt_ref, k_ref, vt_ref, kmean_ref, *, head_dim):
    d = x_ref.shape[-1]
    half = head_dim // 2
    v_rows = head_dim + ONES_ROWS
    nt = (((1,), (1,)), ((), ()))
    q_scale = head_dim ** -0.5 * LOG2E
    lane = lax.broadcasted_iota(jnp.int32, (MOBA_BLOCK, LANES), 1)
    first_half = (lane % head_dim) < half

    for blk in range(qt_ref.shape[1]):
        rows = slice(blk * MOBA_BLOCK, (blk + 1) * MOBA_BLOCK)
        h = _rmsnorm(x_ref[0, rows], g_ref[...]).astype(_BF16)

        qt = lax.dot_general(wqt_ref[...], h, nt, preferred_element_type=_F32)
        cos_t, sin_t = cos_t_ref[blk], sin_t_ref[blk]
        for lo in range(0, d, head_dim):
            x1, x2 = qt[lo:lo + half], qt[lo + half:lo + head_dim]
            qt_ref[0, blk, lo:lo + half, :] = ((x1 * cos_t - x2 * sin_t) * q_scale).astype(_BF16)
            qt_ref[0, blk, lo + half:lo + head_dim, :] = ((x2 * cos_t + x1 * sin_t) * q_scale).astype(_BF16)

        vt = lax.dot_general(wvt_ref[...], h, nt, preferred_element_type=_F32).astype(_BF16)
        for hd in range(d // head_dim):
            vt_ref[0, blk, hd * v_rows:hd * v_rows + head_dim, :] = vt[hd * head_dim:(hd + 1) * head_dim]
            vt_ref[0, blk, hd * v_rows + head_dim:(hd + 1) * v_rows, :] = jnp.ones((ONES_ROWS, MOBA_BLOCK), _BF16)

        cos_k, sin_k = cos_k_ref[rows], sin_k_ref[rows]
        for lo in range(0, d, MXU_COLS):
            k_wide = _dot(h, wk_ref[:, lo:lo + MXU_COLS])
            for sub in range(lo, lo + MXU_COLS, LANES):
                kk = k_wide[:, sub - lo:sub - lo + LANES]
                partner = jnp.where(first_half, pltpu.roll(kk, LANES - half, axis=1), pltpu.roll(kk, half, axis=1))
                roped = kk * cos_k + partner * sin_k
                k_ref[0, rows, sub:sub + LANES] = roped.astype(_BF16)
                kmean_ref[0, blk, :, sub:sub + LANES] = jnp.mean(roped, axis=0, keepdims=True)


def _qkv(x, g, w_qkv, tables):
    bsz, s, d = x.shape
    head_dim = d // N_HEADS
    half = head_dim // 2
    nb = s // MOBA_BLOCK
    wq, wk, wv = jnp.split(w_qkv.astype(_BF16), 3, axis=1)
    cos_k, sin_k, cos_t, sin_t = tables
    nq = min(QKV_BLOCKS, nb)
    tile = pl.BlockSpec((1, nq * MOBA_BLOCK, d), lambda b, t: (b, t, 0))
    tposed = pl.BlockSpec((1, nq, d, MOBA_BLOCK), lambda b, t: (b, t, 0, 0))
    d_aug = N_HEADS * (head_dim + ONES_ROWS)
    v_tposed = pl.BlockSpec((1, nq, d_aug, MOBA_BLOCK), lambda b, t: (b, t, 0, 0))
    k_table = pl.BlockSpec((nq * MOBA_BLOCK, LANES), lambda b, t: (t, 0))
    t_table = pl.BlockSpec((nq, half, MOBA_BLOCK), lambda b, t: (t, 0, 0))
    return pl.pallas_call(
        functools.partial(_qkv_kernel, head_dim=head_dim),
        grid=(bsz, nb // nq),
        in_specs=[tile, _const_spec((1, d)), _const_spec((d, d)), _const_spec((d, d)), _const_spec((d, d)),
                  k_table, k_table, t_table, t_table],
        out_specs=[tposed, tile, v_tposed, pl.BlockSpec((1, nq, 1, d), lambda b, t: (b, t, 0, 0))],
        out_shape=[jax.ShapeDtypeStruct((bsz, nb, d, MOBA_BLOCK), _BF16),
                   jax.ShapeDtypeStruct((bsz, s, d), _BF16),
                   jax.ShapeDtypeStruct((bsz, nb, d_aug, MOBA_BLOCK), _BF16),
                   jax.ShapeDtypeStruct((bsz, nb, 1, d), _F32)],
        compiler_params=pltpu.CompilerParams(
            dimension_semantics=("arbitrary", "arbitrary"), vmem_limit_bytes=VMEM_LIMIT),
        name="qkv_rope",
    )(x, g.reshape(1, d), wq.T, wk, wv.T, cos_k, sin_k, cos_t, sin_t)


def _moba_kernel(qt_ref, k_ref, vt_ref, kmean_ref, causal_ref, o_ref, q_ref, bias_ref, acc_ref, *, head_dim):
    for tile in range(qt_ref.shape[1]):
        _attend_tile(pl.program_id(2) * qt_ref.shape[1] + tile, tile, qt_ref, k_ref, vt_ref, kmean_ref, causal_ref,
                     o_ref, q_ref, bias_ref, acc_ref, head_dim=head_dim)


def _attend_tile(own, tile, qt_ref, k_ref, vt_ref, kmean_ref, causal_ref, o_ref, q_ref, bias_ref, acc_ref, *, head_dim):
    nb = kmean_ref.shape[1]
    blk = MOBA_BLOCK
    width = HEADS_PER_STEP * blk
    v_rows = head_dim + ONES_ROWS

    def head_cols(hh):
        return slice(hh * blk, (hh + 1) * blk)

    def k_block(j):
        return k_ref[0, pl.ds(pl.multiple_of(j * blk, blk), blk), :]

    def v_block(j, hh):
        return vt_ref[0, j, hh * v_rows:(hh + 1) * v_rows, :]

    q_ref[...] = jnp.zeros_like(q_ref)
    for hh in range(HEADS_PER_STEP):
        feats = slice(hh * head_dim, (hh + 1) * head_dim)
        q_ref[feats, head_cols(hh)] = qt_ref[0, tile, feats, :]
    q_all = q_ref[...]

    block_id = lax.broadcasted_iota(jnp.int32, (nb, width), 0)
    gate = jnp.where(block_id < own, _dot(kmean_ref[0].astype(_BF16), q_all), NEG_INF)
    bias = jnp.full((nb, width), NEG_INF, _F32)
    for _ in range(MOBA_TOPK):
        best = jnp.max(gate, axis=0, keepdims=True)
        first = jnp.min(jnp.where(gate == best, block_id, nb), axis=0, keepdims=True)
        picked = block_id == first
        bias = jnp.where(picked, 0.0, bias)
        gate = jnp.where(picked, BELOW_NEG_INF, gate)
    bias_ref[...] = jnp.where(block_id < own, bias, NEG_INF)

    s_own = _dot(k_block(own), q_all) + causal_ref[...]
    own_peak = jnp.max(s_own.reshape(blk // 8, 8, width), axis=0)
    p_own = jnp.exp2(s_own).astype(_BF16)
    for hh in range(HEADS_PER_STEP):
        acc_ref[hh] = _dot(v_block(own, hh), p_own[:, head_cols(hh)])

    def visit_groups(first_block, group, n_groups, peak):
        def visit(i, peak):
            j0 = first_block + i * group
            keys = k_ref[0, pl.ds(pl.multiple_of(j0 * blk, blk), group * blk), :]
            s = _dot(keys, q_ref[...])
            updates = [None] * HEADS_PER_STEP
            for g in range(group):
                t = s[g * blk:(g + 1) * blk] + bias_ref[pl.ds(j0 + g, 1), :]
                peak = jnp.maximum(peak, jnp.max(t.reshape(blk // 8, 8, width), axis=0))
                p = jnp.exp2(t).astype(_BF16)
                for hh in range(HEADS_PER_STEP):
                    u = _dot(v_block(j0 + g, hh), p[:, head_cols(hh)])
                    updates[hh] = u if updates[hh] is None else updates[hh] + u
            for hh in range(HEADS_PER_STEP):
                acc_ref[hh] += updates[hh]
            return peak

        return lax.fori_loop(0, n_groups, visit, peak)

    peak, first, left = own_peak, 0, own
    for size in GROUP_SIZES[:-1]:
        n_groups = left // size
        peak = visit_groups(first, size, n_groups, peak)
        first, left = first + n_groups * size, left - n_groups * size
    peak = visit_groups(first, GROUP_SIZES[-1], (left + GROUP_SIZES[-1] - 1) // GROUP_SIZES[-1], peak)
    query_peak = jnp.max(peak, axis=0, keepdims=True)
    out_of_range = (jnp.max(query_peak) > EXPONENT_GUARD) | (jnp.min(query_peak) < -EXPONENT_GUARD)

    @pl.when(out_of_range)
    def _():
        for hh in range(HEADS_PER_STEP):
            s = _dot(k_block(own), q_ref[:, head_cols(hh)]) + causal_ref[:, head_cols(hh)]
            m_own = jnp.max(s, axis=0, keepdims=True)
            acc = _dot(v_block(own, hh), jnp.exp2(s - m_own).astype(_BF16))

            def visit_exact(j, carry, hh=hh):
                m_run, acc = carry
                t = _dot(k_block(j), q_ref[:, head_cols(hh)]) + bias_ref[pl.ds(j, 1), head_cols(hh)]
                m_new = jnp.maximum(m_run, jnp.max(t, axis=0, keepdims=True))
                acc = jnp.exp2(m_run - m_new) * acc + _dot(v_block(j, hh), jnp.exp2(t - m_new).astype(_BF16))
                return m_new, acc

            _, acc = lax.fori_loop(0, own, visit_exact, (m_own, acc))
            acc_ref[hh] = acc

    o_t = jnp.concatenate([acc_ref[hh, :head_dim] / acc_ref[hh, head_dim:head_dim + 1]
                           for hh in range(HEADS_PER_STEP)], axis=0)
    o_ref[0, tile * blk:(tile + 1) * blk] = o_t.T.astype(_BF16)


def _moba(qt, k, vt, kmean):
    bsz, s, d = k.shape
    head_dim = d // N_HEADS
    nb = s // MOBA_BLOCK
    width = HEADS_PER_STEP * head_dim
    v_rows = head_dim + ONES_ROWS
    pos = jnp.arange(MOBA_BLOCK)
    causal = jnp.tile(jnp.where(pos[:, None] <= pos[None, :], 0.0, NEG_INF).astype(_F32), (1, HEADS_PER_STEP))
    return pl.pallas_call(
        functools.partial(_moba_kernel, head_dim=head_dim),
        grid=(bsz, d // width, nb // ATTN_TILES),
        in_specs=[pl.BlockSpec((1, ATTN_TILES, width, MOBA_BLOCK), lambda b, c, i: (b, i, c, 0)),
                  pl.BlockSpec((1, s, width), lambda b, c, i: (b, 0, c)),
                  pl.BlockSpec((1, nb, HEADS_PER_STEP * v_rows, MOBA_BLOCK), lambda b, c, i: (b, 0, c, 0)),
                  pl.BlockSpec((1, nb, width), lambda b, c, i: (b, 0, c)),
                  pl.BlockSpec(causal.shape, lambda b, c, i: (0, 0))],
        out_specs=pl.BlockSpec((1, ATTN_TILES * MOBA_BLOCK, width), lambda b, c, i: (b, i, c)),
        out_shape=jax.ShapeDtypeStruct((bsz, s, d), _BF16),
        scratch_shapes=[pltpu.VMEM((width, HEADS_PER_STEP * MOBA_BLOCK), _BF16),
                        pltpu.VMEM((nb, HEADS_PER_STEP * MOBA_BLOCK), _F32),
                        pltpu.VMEM((HEADS_PER_STEP, v_rows, MOBA_BLOCK), _F32)],
        compiler_params=pltpu.CompilerParams(
            dimension_semantics=("arbitrary", "arbitrary", "arbitrary"), vmem_limit_bytes=VMEM_LIMIT),
        name="moba_attention",
    )(qt, k, vt, kmean.reshape(bsz, nb, d), causal)


def kernel(x, mix_norm, sc_w_in, sc_w_conv, sc_w_out, moba_w_qkv, moba_w_o, ffn_norm, ffn_w_up, ffn_w_conv,
           ffn_w_down, final_norm):
    bsz, s, d = x.shape
    assert N_HEADS % HEADS_PER_STEP == 0 and (d // N_HEADS * HEADS_PER_STEP) % LANES == 0
    assert s % (MOBA_BLOCK * ATTN_TILES) == 0 and s % (MOBA_BLOCK * QKV_BLOCKS) == 0 and s % min(TM_FFN, s) == 0 and s % min(TM_MIXER, s) == 0
    assert mix_norm.shape[0] == 2 and ffn_w_down.shape[1] % CK_FFN == 0 and d % CK_MIXER == 0

    x = _mixer(x, mix_norm[0], sc_w_in[0], sc_w_conv[0], sc_w_out[0])
    x = _ffn(x, ffn_norm[0], ffn_w_up[0], ffn_w_conv[0], ffn_w_down[0])
    qt, k, vt, kmean = _qkv(x, mix_norm[1], moba_w_qkv[0], _rope_tables(s, d // N_HEADS))
    attn = _moba(qt, k, vt, kmean)
    return _ffn(x, ffn_norm[1], ffn_w_up[1], ffn_w_conv[1], ffn_w_down[1],
                attn=attn, w_o=moba_w_o[0], final_g=final_norm)
```

```python
import functools

import jax
import jax.numpy as jnp
from jax import lax
from jax.experimental import pallas as pl
from jax.experimental.pallas import tpu as pltpu

N_HEADS = 16
MOBA_BLOCK = 256
MOBA_TOPK = 3
ROPE_THETA = 10000.0
RMS_EPS = 1e-6
NEG_INF = -1e30
BELOW_NEG_INF = -3.0e38
LOG2E = 1.4426950408889634
CARRY_ROWS = 8
LANES = 128
MXU_COLS = 256
HEADS_PER_STEP = 4
ONES_ROWS = 16
GROUP_SIZES = (16, 8, 4, 2, 1)
EXPONENT_GUARD = 64.0

ATTN_TILES = 2
QKV_BLOCKS = 4
TM_MIXER = 1024
SUB_MIXER = 256
CK_MIXER = 256
TM_FFN = 1024
SUB_FFN = 256
CK_FFN = 256
VMEM_LIMIT = 56 * 1024 * 1024

_BF16 = jnp.bfloat16
_F32 = jnp.float32


def _dot(a, b):
    return jnp.dot(a, b, preferred_element_type=_F32)


def _rmsnorm(x, g):
    var = jnp.mean(x * x, axis=-1, keepdims=True)
    return x * lax.rsqrt(var + RMS_EPS) * g


def _delay_rows(u, prev, n):
    rolled = pltpu.roll(u, n, axis=0)
    tail = pltpu.roll(prev, n, axis=0)
    row = lax.broadcasted_iota(jnp.int32, prev.shape, 0)
    first = jnp.where(row < n, tail, rolled[:CARRY_ROWS])
    return jnp.concatenate([first, rolled[CARRY_ROWS:]], axis=0)


def _causal_conv3(u, prev, w):
    return _delay_rows(u, prev, 2) * w[0:1] + _delay_rows(u, prev, 1) * w[1:2] + u * w[2:3]


def _reset_carry_at_sequence_start(carry_ref):
    @pl.when(pl.program_id(1) == 0)
    def _():
        carry_ref[...] = jnp.zeros_like(carry_ref)


def _mixer_kernel(x_ref, g_ref, win_ref, wconv_ref, wout_ref, o_ref, carry_ref, *, ck):
    _reset_carry_at_sequence_start(carry_ref)
    d = x_ref.shape[-1]
    for r0 in range(0, x_ref.shape[1], SUB_MIXER):
        rows = slice(r0, r0 + SUB_MIXER)
        x = x_ref[0, rows]
        h = _rmsnorm(x, g_ref[...]).astype(_BF16)

        def in_project(lo):
            bb = _dot(h, win_ref[:, lo:lo + ck])
            cv = _dot(h, win_ref[:, d + lo:d + lo + ck]) * _dot(h, win_ref[:, 2 * d + lo:2 * d + lo + ck])
            prev = carry_ref[:, lo:lo + ck]
            carry_ref[:, lo:lo + ck] = cv[-CARRY_ROWS:, :]
            return bb, cv, prev

        chunks = list(range(0, d, ck))
        acc = x
        ahead = in_project(chunks[0])
        for c, lo in enumerate(chunks):
            bb, cv, prev = ahead
            z = (bb * _causal_conv3(cv, prev, wconv_ref[:, lo:lo + ck])).astype(_BF16)
            if c + 1 < len(chunks):
                ahead = in_project(chunks[c + 1])
            acc = acc + _dot(z, wout_ref[lo:lo + ck, :])
        o_ref[0, rows] = acc


def _const_spec(shape):
    zeros = (0,) * len(shape)
    return pl.BlockSpec(shape, lambda *_: zeros, pipeline_mode=pl.Buffered(1))


def _mixer(x, g, w_in, w_conv, w_out):
    bsz, s, d = x.shape
    tm = min(TM_MIXER, s)
    tile = pl.BlockSpec((1, tm, d), lambda b, t: (b, t, 0))
    return pl.pallas_call(
        functools.partial(_mixer_kernel, ck=CK_MIXER),
        grid=(bsz, s // tm),
        in_specs=[tile, _const_spec((1, d)), _const_spec((d, 3 * d)), _const_spec((3, d)), _const_spec((d, d))],
        out_specs=tile,
        out_shape=jax.ShapeDtypeStruct(x.shape, _F32),
        scratch_shapes=[pltpu.VMEM((CARRY_ROWS, d), _F32)],
        compiler_params=pltpu.CompilerParams(
            dimension_semantics=("arbitrary", "arbitrary"), vmem_limit_bytes=VMEM_LIMIT),
        name="mixer",
    )(x, g.reshape(1, d), w_in.astype(_BF16), w_conv, w_out.astype(_BF16))


def _ffn_body(x, g_ref, wup_ref, wconv_ref, wdown_ref, carry_ref, *, d_ff, ck):
    h = _rmsnorm(x, g_ref[...]).astype(_BF16)

    def up_project(lo):
        halves = []
        for col in (lo, d_ff + lo):
            up = _dot(h, wup_ref[:, col:col + ck])
            halves.append((up, carry_ref[:, col:col + ck]))
            carry_ref[:, col:col + ck] = up[-CARRY_ROWS:, :]
        return halves

    chunks = list(range(0, d_ff, ck))
    acc = x
    ahead = up_project(chunks[0])
    for c, lo in enumerate(chunks):
        gate, lin = (_causal_conv3(up, prev, wconv_ref[:, col:col + ck])
                     for (up, prev), col in zip(ahead, (lo, d_ff + lo)))
        act = (gate / (1.0 + jnp.exp(-gate)) * lin).astype(_BF16)
        if c + 1 < len(chunks):
            ahead = up_project(chunks[c + 1])
        acc = acc + _dot(act, wdown_ref[lo:lo + ck, :])
    return acc


def _ffn_kernel(x_ref, g_ref, wup_ref, wconv_ref, wdown_ref, o_ref, carry_ref, *, d_ff, ck):
    _reset_carry_at_sequence_start(carry_ref)
    for lo in range(0, x_ref.shape[1], SUB_FFN):
        rows = slice(lo, lo + SUB_FFN)
        o_ref[0, rows] = _ffn_body(x_ref[0, rows], g_ref, wup_ref, wconv_ref, wdown_ref, carry_ref, d_ff=d_ff, ck=ck)


def _proj_ffn_norm_kernel(x_ref, a_ref, wo_ref, g_ref, wup_ref, wconv_ref, wdown_ref, fg_ref, o_ref, carry_ref,
                          *, d_ff, ck):
    _reset_carry_at_sequence_start(carry_ref)
    for lo in range(0, x_ref.shape[1], SUB_FFN):
        rows = slice(lo, lo + SUB_FFN)
        x = x_ref[0, rows] + _dot(a_ref[0, rows], wo_ref[...])
        y = _ffn_body(x, g_ref, wup_ref, wconv_ref, wdown_ref, carry_ref, d_ff=d_ff, ck=ck)
        o_ref[0, rows] = _rmsnorm(y, fg_ref[...])


def _ffn(x, g, w_up, w_conv, w_down, attn=None, w_o=None, final_g=None):
    bsz, s, d = x.shape
    d_ff = w_down.shape[0]
    tm = min(TM_FFN, s)
    tile = pl.BlockSpec((1, tm, d), lambda b, t: (b, t, 0))
    ffn_specs = [_const_spec((1, d)), _const_spec((d, 2 * d_ff)), _const_spec((3, 2 * d_ff)), _const_spec((d_ff, d))]
    ffn_args = [g.reshape(1, d), w_up.astype(_BF16), w_conv, w_down.astype(_BF16)]
    if attn is None:
        body, name = _ffn_kernel, "ffn"
        in_specs, args = [tile] + ffn_specs, [x] + ffn_args
    else:
        body, name = _proj_ffn_norm_kernel, "proj_ffn_norm"
        in_specs = [tile, tile, _const_spec((d, d))] + ffn_specs + [_const_spec((1, d))]
        args = [x, attn, w_o.astype(_BF16)] + ffn_args + [final_g.reshape(1, d)]
    return pl.pallas_call(
        functools.partial(body, d_ff=d_ff, ck=CK_FFN),
        grid=(bsz, s // tm),
        in_specs=in_specs,
        out_specs=tile,
        out_shape=jax.ShapeDtypeStruct(x.shape, _F32),
        scratch_shapes=[pltpu.VMEM((CARRY_ROWS, 2 * d_ff), _F32)],
        compiler_params=pltpu.CompilerParams(
            dimension_semantics=("arbitrary", "arbitrary"), vmem_limit_bytes=VMEM_LIMIT),
        name=name,
    )(*args)


def _rope_table_kernel(inv_row_ref, sign_row_ref, inv_col_ref, cos_k_ref, sin_k_ref, cos_t_ref, sin_t_ref):
    base = pl.program_id(0) * MOBA_BLOCK
    pos_rows = (base + lax.broadcasted_iota(jnp.int32, cos_k_ref.shape, 0)).astype(_F32)
    ang = pos_rows * inv_row_ref[...]
    cos_k_ref[...] = jnp.cos(ang)
    sin_k_ref[...] = jnp.sin(ang) * sign_row_ref[...]
    pos_cols = (base + lax.broadcasted_iota(jnp.int32, cos_t_ref.shape[1:], 1)).astype(_F32)
    ang_t = pos_cols * inv_col_ref[...]
    cos_t_ref[0] = jnp.cos(ang_t)
    sin_t_ref[0] = jnp.sin(ang_t)


def _rope_tables(s, head_dim):
    half = head_dim // 2
    nb = s // MOBA_BLOCK
    inv = ROPE_THETA ** (-jnp.arange(half, dtype=_F32) / half)
    inv_row = jnp.tile(inv, LANES // half).reshape(1, LANES)
    sign_row = jnp.tile(jnp.concatenate([-jnp.ones(half, _F32), jnp.ones(half, _F32)]),
                        LANES // head_dim).reshape(1, LANES)
    inv_col = jnp.broadcast_to(inv[:, None], (half, MOBA_BLOCK))
    return pl.pallas_call(
        _rope_table_kernel,
        grid=(nb,),
        in_specs=[pl.BlockSpec((1, LANES), lambda t: (0, 0)), pl.BlockSpec((1, LANES), lambda t: (0, 0)),
                  pl.BlockSpec((half, MOBA_BLOCK), lambda t: (0, 0))],
        out_specs=[pl.BlockSpec((MOBA_BLOCK, LANES), lambda t: (t, 0)),
                   pl.BlockSpec((MOBA_BLOCK, LANES), lambda t: (t, 0)),
                   pl.BlockSpec((1, half, MOBA_BLOCK), lambda t: (t, 0, 0)),
                   pl.BlockSpec((1, half, MOBA_BLOCK), lambda t: (t, 0, 0))],
        out_shape=[jax.ShapeDtypeStruct((s, LANES), _F32), jax.ShapeDtypeStruct((s, LANES), _F32),
                   jax.ShapeDtypeStruct((nb, half, MOBA_BLOCK), _F32),
                   jax.ShapeDtypeStruct((nb, half, MOBA_BLOCK), _F32)],
        name="rope_tables",
    )(inv_row, sign_row, inv_col)


def _qkv_kernel(x_ref, g_ref, wqt_ref, wk_ref, wvt_ref, cos_k_ref, sin_k_ref, cos_t_ref, sin_t_ref,
                qt_ref, k_ref, vt_ref, kmean_ref, *, head_dim):
    d = x_ref.shape[-1]
    half = head_dim // 2
    v_rows = head_dim + ONES_ROWS
    nt = (((1,), (1,)), ((), ()))
    q_scale = head_dim ** -0.5 * LOG2E
    lane = lax.broadcasted_iota(jnp.int32, (MOBA_BLOCK, LANES), 1)
    first_half = (lane % head_dim) < half

    for blk in range(qt_ref.shape[1]):
        rows = slice(blk * MOBA_BLOCK, (blk + 1) * MOBA_BLOCK)
        h = _rmsnorm(x_ref[0, rows], g_ref[...]).astype(_BF16)

        qt = lax.dot_general(wqt_ref[...], h, nt, preferred_element_type=_F32)
        cos_t, sin_t = cos_t_ref[blk], sin_t_ref[blk]
        for lo in range(0, d, head_dim):
            x1, x2 = qt[lo:lo + half], qt[lo + half:lo + head_dim]
            qt_ref[0, blk, lo:lo + half, :] = ((x1 * cos_t - x2 * sin_t) * q_scale).astype(_BF16)
            qt_ref[0, blk, lo + half:lo + head_dim, :] = ((x2 * cos_t + x1 * sin_t) * q_scale).astype(_BF16)

        vt = lax.dot_general(wvt_ref[...], h, nt, preferred_element_type=_F32).astype(_BF16)
        for hd in range(d // head_dim):
            vt_ref[0, blk, hd * v_rows:hd * v_rows + head_dim, :] = vt[hd * head_dim:(hd + 1) * head_dim]
            vt_ref[0, blk, hd * v_rows + head_dim:(hd + 1) * v_rows, :] = jnp.ones((ONES_ROWS, MOBA_BLOCK), _BF16)

        cos_k, sin_k = cos_k_ref[rows], sin_k_ref[rows]
        for lo in range(0, d, MXU_COLS):
            k_wide = _dot(h, wk_ref[:, lo:lo + MXU_COLS])
            for sub in range(lo, lo + MXU_COLS, LANES):
                kk = k_wide[:, sub - lo:sub - lo + LANES]
                partner = jnp.where(first_half, pltpu.roll(kk, LANES - half, axis=1), pltpu.roll(kk, half, axis=1))
                roped = kk * cos_k + partner * sin_k
                k_ref[0, rows, sub:sub + LANES] = roped.astype(_BF16)
                kmean_ref[0, blk, :, sub:sub + LANES] = jnp.mean(roped, axis=0, keepdims=True)


def _qkv(x, g, w_qkv, tables):
    bsz, s, d = x.shape
    head_dim = d // N_HEADS
    half = head_dim // 2
    nb = s // MOBA_BLOCK
    wq, wk, wv = jnp.split(w_qkv.astype(_BF16), 3, axis=1)
    cos_k, sin_k, cos_t, sin_t = tables
    nq = min(QKV_BLOCKS, nb)
    tile = pl.BlockSpec((1, nq * MOBA_BLOCK, d), lambda b, t: (b, t, 0))
    tposed = pl.BlockSpec((1, nq, d, MOBA_BLOCK), lambda b, t: (b, t, 0, 0))
    d_aug = N_HEADS * (head_dim + ONES_ROWS)
    v_tposed = pl.BlockSpec((1, nq, d_aug, MOBA_BLOCK), lambda b, t: (b, t, 0, 0))
    k_table = pl.BlockSpec((nq * MOBA_BLOCK, LANES), lambda b, t: (t, 0))
    t_table = pl.BlockSpec((nq, half, MOBA_BLOCK), lambda b, t: (t, 0, 0))
    return pl.pallas_call(
        functools.partial(_qkv_kernel, head_dim=head_dim),
        grid=(bsz, nb // nq),
        in_specs=[tile, _const_spec((1, d)), _const_spec((d, d)), _const_spec((d, d)), _const_spec((d, d)),
                  k_table, k_table, t_table, t_table],
        out_specs=[tposed, tile, v_tposed, pl.BlockSpec((1, nq, 1, d), lambda b, t: (b, t, 0, 0))],
        out_shape=[jax.ShapeDtypeStruct((bsz, nb, d, MOBA_BLOCK), _BF16),
                   jax.ShapeDtypeStruct((bsz, s, d), _BF16),
                   jax.ShapeDtypeStruct((bsz, nb, d_aug, MOBA_BLOCK), _BF16),
                   jax.ShapeDtypeStruct((bsz, nb, 1, d), _F32)],
        compiler_params=pltpu.CompilerParams(
            dimension_semantics=("arbitrary", "arbitrary"), vmem_limit_bytes=VMEM_LIMIT),
        name="qkv_rope",
    )(x, g.reshape(1, d), wq.T, wk, wv.T, cos_k, sin_k, cos_t, sin_t)


def _moba_kernel(qt_ref, k_ref, vt_ref, kmean_ref, causal_ref, o_ref, q_ref, bias_ref, acc_ref, *, head_dim):
    for tile in range(qt_ref.shape[1]):
        _attend_tile(pl.program_id(2) * qt_ref.shape[1] + tile, tile, qt_ref, k_ref, vt_ref, kmean_ref, causal_ref,
                     o_ref, q_ref, bias_ref, acc_ref, head_dim=head_dim)


def _attend_tile(own, tile, qt_ref, k_ref, vt_ref, kmean_ref, causal_ref, o_ref, q_ref, bias_ref, acc_ref, *, head_dim):
    nb = kmean_ref.shape[1]
    blk = MOBA_BLOCK
    width = HEADS_PER_STEP * blk
    v_rows = head_dim + ONES_ROWS

    def head_cols(hh):
        return slice(hh * blk, (hh + 1) * blk)

    def k_block(j):
        return k_ref[0, pl.ds(pl.multiple_of(j * blk, blk), blk), :]

    def v_block(j, hh):
        return vt_ref[0, j, hh * v_rows:(hh + 1) * v_rows, :]

    q_ref[...] = jnp.zeros_like(q_ref)
    for hh in range(HEADS_PER_STEP):
        feats = slice(hh * head_dim, (hh + 1) * head_dim)
        q_ref[feats, head_cols(hh)] = qt_ref[0, tile, feats, :]
    q_all = q_ref[...]

    block_id = lax.broadcasted_iota(jnp.int32, (nb, width), 0)
    gate = jnp.where(block_id < own, _dot(kmean_ref[0].astype(_BF16), q_all), NEG_INF)
    bias = jnp.full((nb, width), NEG_INF, _F32)
    for _ in range(MOBA_TOPK):
        best = jnp.max(gate, axis=0, keepdims=True)
        first = jnp.min(jnp.where(gate == best, block_id, nb), axis=0, keepdims=True)
        picked = block_id == first
        bias = jnp.where(picked, 0.0, bias)
        gate = jnp.where(picked, BELOW_NEG_INF, gate)
    bias_ref[...] = jnp.where(block_id < own, bias, NEG_INF)

    s_own = _dot(k_block(own), q_all) + causal_ref[...]
    own_peak = jnp.max(s_own.reshape(blk // 8, 8, width), axis=0)
    p_own = jnp.exp2(s_own).astype(_BF16)
    for hh in range(HEADS_PER_STEP):
        acc_ref[hh] = _dot(v_block(own, hh), p_own[:, head_cols(hh)])

    def visit_groups(first_block, group, n_groups, peak):
        def visit(i, peak):
            j0 = first_block + i * group
            keys = k_ref[0, pl.ds(pl.multiple_of(j0 * blk, blk), group * blk), :]
            s = _dot(keys, q_ref[...])
            updates = [None] * HEADS_PER_STEP
            for g in range(group):
                t = s[g * blk:(g + 1) * blk] + bias_ref[pl.ds(j0 + g, 1), :]
                peak = jnp.maximum(peak, jnp.max(t.reshape(blk // 8, 8, width), axis=0))
                p = jnp.exp2(t).astype(_BF16)
                for hh in range(HEADS_PER_STEP):
                    u = _dot(v_block(j0 + g, hh), p[:, head_cols(hh)])
                    updates[hh] = u if updates[hh] is None else updates[hh] + u
            for hh in range(HEADS_PER_STEP):
                acc_ref[hh] += updates[hh]
            return peak

        return lax.fori_loop(0, n_groups, visit, peak)

    peak, first, left = own_peak, 0, own
    for size in GROUP_SIZES[:-1]:
        n_groups = left // size
        peak = visit_groups(first, size, n_groups, peak)
        first, left = first + n_groups * size, left - n_groups * size
    peak = visit_groups(first, GROUP_SIZES[-1], (left + GROUP_SIZES[-1] - 1) // GROUP_SIZES[-1], peak)
    query_peak = jnp.max(peak, axis=0, keepdims=True)
    out_of_range = (jnp.max(query_peak) > EXPONENT_GUARD) | (jnp.min(query_peak) < -EXPONENT_GUARD)

    @pl.when(out_of_range)
    def _():
        for hh in range(HEADS_PER_STEP):
            s = _dot(k_block(own), q_ref[:, head_cols(hh)]) + causal_ref[:, head_cols(hh)]
            m_own = jnp.max(s, axis=0, keepdims=True)
            acc = _dot(v_block(own, hh), jnp.exp2(s - m_own).astype(_BF16))

            def visit_exact(j, carry, hh=hh):
                m_run, acc = carry
                t = _dot(k_block(j), q_ref[:, head_cols(hh)]) + bias_ref[pl.ds(j, 1), head_cols(hh)]
                m_new = jnp.maximum(m_run, jnp.max(t, axis=0, keepdims=True))
                acc = jnp.exp2(m_run - m_new) * acc + _dot(v_block(j, hh), jnp.exp2(t - m_new).astype(_BF16))
                return m_new, acc

            _, acc = lax.fori_loop(0, own, visit_exact, (m_own, acc))
            acc_ref[hh] = acc

    o_t = jnp.concatenate([acc_ref[hh, :head_dim] / acc_ref[hh, head_dim:head_dim + 1]
                           for hh in range(HEADS_PER_STEP)], axis=0)
    o_ref[0, tile * blk:(tile + 1) * blk] = o_t.T.astype(_BF16)


def _moba(qt, k, vt, kmean):
    bsz, s, d = k.shape
    head_dim = d // N_HEADS
    nb = s // MOBA_BLOCK
    width = HEADS_PER_STEP * head_dim
    v_rows = head_dim + ONES_ROWS
    pos = jnp.arange(MOBA_BLOCK)
    causal = jnp.tile(jnp.where(pos[:, None] <= pos[None, :], 0.0, NEG_INF).astype(_F32), (1, HEADS_PER_STEP))
    return pl.pallas_call(
        functools.partial(_moba_kernel, head_dim=head_dim),
        grid=(bsz, d // width, nb // ATTN_TILES),
        in_specs=[pl.BlockSpec((1, ATTN_TILES, width, MOBA_BLOCK), lambda b, c, i: (b, i, c, 0)),
                  pl.BlockSpec((1, s, width), lambda b, c, i: (b, 0, c)),
                  pl.BlockSpec((1, nb, HEADS_PER_STEP * v_rows, MOBA_BLOCK), lambda b, c, i: (b, 0, c, 0)),
                  pl.BlockSpec((1, nb, width), lambda b, c, i: (b, 0, c)),
                  pl.BlockSpec(causal.shape, lambda b, c, i: (0, 0))],
        out_specs=pl.BlockSpec((1, ATTN_TILES * MOBA_BLOCK, width), lambda b, c, i: (b, i, c)),
        out_shape=jax.ShapeDtypeStruct((bsz, s, d), _BF16),
        scratch_shapes=[pltpu.VMEM((width, HEADS_PER_STEP * MOBA_BLOCK), _BF16),
                        pltpu.VMEM((nb, HEADS_PER_STEP * MOBA_BLOCK), _F32),
                        pltpu.VMEM((HEADS_PER_STEP, v_rows, MOBA_BLOCK), _F32)],
        compiler_params=pltpu.CompilerParams(
            dimension_semantics=("arbitrary", "arbitrary", "arbitrary"), vmem_limit_bytes=VMEM_LIMIT),
        name="moba_attention",
    )(qt, k, vt, kmean.reshape(bsz, nb, d), causal)


def kernel(x, mix_norm, sc_w_in, sc_w_conv, sc_w_out, moba_w_qkv, moba_w_o, ffn_norm, ffn_w_up, ffn_w_conv,
           ffn_w_down, final_norm):
    bsz, s, d = x.shape
    assert N_HEADS % HEADS_PER_STEP == 0 and (d // N_HEADS * HEADS_PER_STEP) % LANES == 0
    assert s % (MOBA_BLOCK * ATTN_TILES) == 0 and s % (MOBA_BLOCK * QKV_BLOCKS) == 0
    assert s % min(TM_FFN, s) == 0 and s % min(TM_MIXER, s) == 0
    assert mix_norm.shape[0] == 2 and ffn_w_down.shape[1] % CK_FFN == 0 and d % CK_MIXER == 0

    x = _mixer(x, mix_norm[0], sc_w_in[0], sc_w_conv[0], sc_w_out[0])
    x = _ffn(x, ffn_norm[0], ffn_w_up[0], ffn_w_conv[0], ffn_w_down[0])
    qt, k, vt, kmean = _qkv(x, mix_norm[1], moba_w_qkv[0], _rope_tables(s, d // N_HEADS))
    attn = _moba(qt, k, vt, kmean)
    return _ffn(x, ffn_norm[1], ffn_w_up[1], ffn_w_conv[1], ffn_w_down[1],
                attn=attn, w_o=moba_w_o[0], final_g=final_norm)
```

```python
import functools

import jax
import jax.numpy as jnp
from jax import lax
from jax.experimental import pallas as pl
from jax.experimental.pallas import tpu as pltpu

N_HEADS = 16
MOBA_BLOCK = 256
MOBA_TOPK = 3
ROPE_THETA = 10000.0
RMS_EPS = 1e-6
NEG_INF = -1e30
BELOW_NEG_INF = -3.0e38
LOG2E = 1.4426950408889634
CARRY_ROWS = 8
LANES = 128
MXU_COLS = 256
HEADS_PER_STEP = 4
ONES_ROWS = 16
GROUP_SIZES = (16, 8, 4, 2, 1)
EXPONENT_GUARD = 64.0

ATTN_TILES = 2
QKV_BLOCKS = 4
TM_MIXER = 1024
SUB_MIXER = 256
CK_MIXER = 256
TM_FFN = 1024
SUB_FFN = 256
CK_FFN = 256
VMEM_LIMIT = 56 * 1024 * 1024

_BF16 = jnp.bfloat16
_F32 = jnp.float32


def _dot(a, b):
    return jnp.dot(a, b, preferred_element_type=_F32)


def _rmsnorm(x, g):
    var = jnp.mean(x * x, axis=-1, keepdims=True)
    return x * lax.rsqrt(var + RMS_EPS) * g


def _delay_rows(u, prev, n):
    rolled = pltpu.roll(u, n, axis=0)
    tail = pltpu.roll(prev, n, axis=0)
    row = lax.broadcasted_iota(jnp.int32, prev.shape, 0)
    first = jnp.where(row < n, tail, rolled[:CARRY_ROWS])
    return jnp.concatenate([first, rolled[CARRY_ROWS:]], axis=0)


def _causal_conv3(u, prev, w):
    return _delay_rows(u, prev, 2) * w[0:1] + _delay_rows(u, prev, 1) * w[1:2] + u * w[2:3]


def _reset_carry_at_sequence_start(carry_ref):
    @pl.when(pl.program_id(1) == 0)
    def _():
        carry_ref[...] = jnp.zeros_like(carry_ref)


def _mixer_kernel(x_ref, g_ref, win_ref, wconv_ref, wout_ref, o_ref, carry_ref, *, ck):
    _reset_carry_at_sequence_start(carry_ref)
    d = x_ref.shape[-1]
    for r0 in range(0, x_ref.shape[1], SUB_MIXER):
        rows = slice(r0, r0 + SUB_MIXER)
        x = x_ref[0, rows]
        h = _rmsnorm(x, g_ref[...]).astype(_BF16)

        def in_project(lo):
            bb = _dot(h, win_ref[:, lo:lo + ck])
            cv = _dot(h, win_ref[:, d + lo:d + lo + ck]) * _dot(h, win_ref[:, 2 * d + lo:2 * d + lo + ck])
            prev = carry_ref[:, lo:lo + ck]
            carry_ref[:, lo:lo + ck] = cv[-CARRY_ROWS:, :]
            return bb, cv, prev

        chunks = list(range(0, d, ck))
        acc = x
        ahead = in_project(chunks[0])
        for c, lo in enumerate(chunks):
            bb, cv, prev = ahead
            z = (bb * _causal_conv3(cv, prev, wconv_ref[:, lo:lo + ck])).astype(_BF16)
            if c + 1 < len(chunks):
                ahead = in_project(chunks[c + 1])
            acc = acc + _dot(z, wout_ref[lo:lo + ck, :])
        o_ref[0, rows] = acc


def _const_spec(shape):
    zeros = (0,) * len(shape)
    return pl.BlockSpec(shape, lambda *_: zeros, pipeline_mode=pl.Buffered(1))


def _mixer(x, g, w_in, w_conv, w_out):
    bsz, s, d = x.shape
    tm = min(TM_MIXER, s)
    tile = pl.BlockSpec((1, tm, d), lambda b, t: (b, t, 0))
    return pl.pallas_call(
        functools.partial(_mixer_kernel, ck=CK_MIXER),
        grid=(bsz, s // tm),
        in_specs=[tile, _const_spec((1, d)), _const_spec((d, 3 * d)), _const_spec((3, d)), _const_spec((d, d))],
        out_specs=tile,
        out_shape=jax.ShapeDtypeStruct(x.shape, _F32),
        scratch_shapes=[pltpu.VMEM((CARRY_ROWS, d), _F32)],
        compiler_params=pltpu.CompilerParams(
            dimension_semantics=("arbitrary", "arbitrary"), vmem_limit_bytes=VMEM_LIMIT),
        name="mixer",
    )(x, g.reshape(1, d), w_in.astype(_BF16), w_conv, w_out.astype(_BF16))


def _ffn_body(x, g_ref, wup_ref, wconv_ref, wdown_ref, carry_ref, *, d_ff, ck):
    h = _rmsnorm(x, g_ref[...]).astype(_BF16)

    def up_project(lo):
        halves = []
        for col in (lo, d_ff + lo):
            up = _dot(h, wup_ref[:, col:col + ck])
            halves.append((up, carry_ref[:, col:col + ck]))
            carry_ref[:, col:col + ck] = up[-CARRY_ROWS:, :]
        return halves

    chunks = list(range(0, d_ff, ck))
    acc = x
    ahead = up_project(chunks[0])
    for c, lo in enumerate(chunks):
        gate, lin = (_causal_conv3(up, prev, wconv_ref[:, col:col + ck])
                     for (up, prev), col in zip(ahead, (lo, d_ff + lo)))
        act = (gate / (1.0 + jnp.exp(-gate)) * lin).astype(_BF16)
        if c + 1 < len(chunks):
            ahead = up_project(chunks[c + 1])
        acc = acc + _dot(act, wdown_ref[lo:lo + ck, :])
    return acc


def _ffn_kernel(x_ref, g_ref, wup_ref, wconv_ref, wdown_ref, o_ref, carry_ref, *, d_ff, ck):
    _reset_carry_at_sequence_start(carry_ref)
    for lo in range(0, x_ref.shape[1], SUB_FFN):
        rows = slice(lo, lo + SUB_FFN)
        o_ref[0, rows] = _ffn_body(x_ref[0, rows], g_ref, wup_ref, wconv_ref, wdown_ref, carry_ref, d_ff=d_ff, ck=ck)


def _proj_ffn_norm_kernel(x_ref, a_ref, wo_ref, g_ref, wup_ref, wconv_ref, wdown_ref, fg_ref, o_ref, carry_ref,
                          *, d_ff, ck):
    _reset_carry_at_sequence_start(carry_ref)
    for lo in range(0, x_ref.shape[1], SUB_FFN):
        rows = slice(lo, lo + SUB_FFN)
        x = x_ref[0, rows] + _dot(a_ref[0, rows], wo_ref[...])
        y = _ffn_body(x, g_ref, wup_ref, wconv_ref, wdown_ref, carry_ref, d_ff=d_ff, ck=ck)
        o_ref[0, rows] = _rmsnorm(y, fg_ref[...])


def _ffn(x, g, w_up, w_conv, w_down, attn=None, w_o=None, final_g=None):
    bsz, s, d = x.shape
    d_ff = w_down.shape[0]
    tm = min(TM_FFN, s)
    tile = pl.BlockSpec((1, tm, d), lambda b, t: (b, t, 0))
    ffn_specs = [_const_spec((1, d)), _const_spec((d, 2 * d_ff)), _const_spec((3, 2 * d_ff)), _const_spec((d_ff, d))]
    ffn_args = [g.reshape(1, d), w_up.astype(_BF16), w_conv, w_down.astype(_BF16)]
    if attn is None:
        body, name = _ffn_kernel, "ffn"
        in_specs, args = [tile] + ffn_specs, [x] + ffn_args
    else:
        body, name = _proj_ffn_norm_kernel, "proj_ffn_norm"
        in_specs = [tile, tile, _const_spec((d, d))] + ffn_specs + [_const_spec((1, d))]
        args = [x, attn, w_o.astype(_BF16)] + ffn_args + [final_g.reshape(1, d)]
    return pl.pallas_call(
        functools.partial(body, d_ff=d_ff, ck=CK_FFN),
        grid=(bsz, s // tm),
        in_specs=in_specs,
        out_specs=tile,
        out_shape=jax.ShapeDtypeStruct(x.shape, _F32),
        scratch_shapes=[pltpu.VMEM((CARRY_ROWS, 2 * d_ff), _F32)],
        compiler_params=pltpu.CompilerParams(
            dimension_semantics=("arbitrary", "arbitrary"), vmem_limit_bytes=VMEM_LIMIT),
        name=name,
    )(*args)


def _rope_table_kernel(inv_row_ref, sign_row_ref, inv_col_ref, cos_k_ref, sin_k_ref, cos_t_ref, sin_t_ref):
    base = pl.program_id(0) * MOBA_BLOCK
    pos_rows = (base + lax.broadcasted_iota(jnp.int32, cos_k_ref.shape, 0)).astype(_F32)
    ang = pos_rows * inv_row_ref[...]
    cos_k_ref[...] = jnp.cos(ang)
    sin_k_ref[...] = jnp.sin(ang) * sign_row_ref[...]
    pos_cols = (base + lax.broadcasted_iota(jnp.int32, cos_t_ref.shape[1:], 1)).astype(_F32)
    ang_t = pos_cols * inv_col_ref[...]
    cos_t_ref[0] = jnp.cos(ang_t)
    sin_t_ref[0] = jnp.sin(ang_t)


def _rope_tables(s, head_dim):
    half = head_dim // 2
    nb = s // MOBA_BLOCK
    inv = ROPE_THETA ** (-jnp.arange(half, dtype=_F32) / half)
    inv_row = jnp.tile(inv, LANES // half).reshape(1, LANES)
    sign_row = jnp.tile(jnp.concatenate([-jnp.ones(half, _F32), jnp.ones(half, _F32)]),
                        LANES // head_dim).reshape(1, LANES)
    inv_col = jnp.broadcast_to(inv[:, None], (half, MOBA_BLOCK))
    return pl.pallas_call(
        _rope_table_kernel,
        grid=(nb,),
        in_specs=[pl.BlockSpec((1, LANES), lambda t: (0, 0)), pl.BlockSpec((1, LANES), lambda t: (0, 0)),
                  pl.BlockSpec((half, MOBA_BLOCK), lambda t: (0, 0))],
        out_specs=[pl.BlockSpec((MOBA_BLOCK, LANES), lambda t: (t, 0)),
                   pl.BlockSpec((MOBA_BLOCK, LANES), lambda t: (t, 0)),
                   pl.BlockSpec((1, half, MOBA_BLOCK), lambda t: (t, 0, 0)),
                   pl.BlockSpec((1, half, MOBA_BLOCK), lambda t: (t, 0, 0))],
        out_shape=[jax.ShapeDtypeStruct((s, LANES), _F32), jax.ShapeDtypeStruct((s, LANES), _F32),
                   jax.ShapeDtypeStruct((nb, half, MOBA_BLOCK), _F32),
                   jax.ShapeDtypeStruct((nb, half, MOBA_BLOCK), _F32)],
        name="rope_tables",
    )(inv_row, sign_row, inv_col)


def _qkv_kernel(x_ref, g_ref, wqt_ref, wk_ref, wvt_ref, cos_k_ref, sin_k_ref, cos_t_ref, sin_t_ref,
                qt_ref, k_ref, vt_ref, kmean_ref, *, head_dim):
    d = x_ref.shape[-1]
    half = head_dim // 2
    v_rows = head_dim + ONES_ROWS
    nt = (((1,), (1,)), ((), ()))
    q_scale = head_dim ** -0.5 * LOG2E
    lane = lax.broadcasted_iota(jnp.int32, (MOBA_BLOCK, LANES), 1)
    first_half = (lane % head_dim) < half

    for blk in range(qt_ref.shape[1]):
        rows = slice(blk * MOBA_BLOCK, (blk + 1) * MOBA_BLOCK)
        h = _rmsnorm(x_ref[0, rows], g_ref[...]).astype(_BF16)

        qt = lax.dot_general(wqt_ref[...], h, nt, preferred_element_type=_F32)
        cos_t, sin_t = cos_t_ref[blk], sin_t_ref[blk]
        for lo in range(0, d, head_dim):
            x1, x2 = qt[lo:lo + half], qt[lo + half:lo + head_dim]
            qt_ref[0, blk, lo:lo + half, :] = ((x1 * cos_t - x2 * sin_t) * q_scale).astype(_BF16)
            qt_ref[0, blk, lo + half:lo + head_dim, :] = ((x2 * cos_t + x1 * sin_t) * q_scale).astype(_BF16)

        vt = lax.dot_general(wvt_ref[...], h, nt, preferred_element_type=_F32).astype(_BF16)
        for hd in range(d // head_dim):
            vt_ref[0, blk, hd * v_rows:hd * v_rows + head_dim, :] = vt[hd * head_dim:(hd + 1) * head_dim]
            vt_ref[0, blk, hd * v_rows + head_dim:(hd + 1) * v_rows, :] = jnp.ones((ONES_ROWS, MOBA_BLOCK), _BF16)

        cos_k, sin_k = cos_k_ref[rows], sin_k_ref[rows]
        for lo in range(0, d, MXU_COLS):
            k_wide = _dot(h, wk_ref[:, lo:lo + MXU_COLS])
            for sub in range(lo, lo + MXU_COLS, LANES):
                kk = k_wide[:, sub - lo:sub - lo + LANES]
                partner = jnp.where(first_half, pltpu.roll(kk, LANES - half, axis=1), pltpu.roll(kk, half, axis=1))
                roped = kk * cos_k + partner * sin_k
                k_ref[0, rows, sub:sub + LANES] = roped.astype(_BF16)
                kmean_ref[0, blk, :, sub:sub + LANES] = jnp.mean(roped, axis=0, keepdims=True)


def _qkv(x, g, w_qkv, tables):
    bsz, s, d = x.shape
    head_dim = d // N_HEADS
    half = head_dim // 2
    nb = s // MOBA_BLOCK
    wq, wk, wv = jnp.split(w_qkv.astype(_BF16), 3, axis=1)
    cos_k, sin_k, cos_t, sin_t = tables
    nq = min(QKV_BLOCKS, nb)
    tile = pl.BlockSpec((1, nq * MOBA_BLOCK, d), lambda b, t: (b, t, 0))
    tposed = pl.BlockSpec((1, nq, d, MOBA_BLOCK), lambda b, t: (b, t, 0, 0))
    d_aug = N_HEADS * (head_dim + ONES_ROWS)
    v_tposed = pl.BlockSpec((1, nq, d_aug, MOBA_BLOCK), lambda b, t: (b, t, 0, 0))
    k_table = pl.BlockSpec((nq * MOBA_BLOCK, LANES), lambda b, t: (t, 0))
    t_table = pl.BlockSpec((nq, half, MOBA_BLOCK), lambda b, t: (t, 0, 0))
    return pl.pallas_call(
        functools.partial(_qkv_kernel, head_dim=head_dim),
        grid=(bsz, nb // nq),
        in_specs=[tile, _const_spec((1, d)), _const_spec((d, d)), _const_spec((d, d)), _const_spec((d, d)),
                  k_table, k_table, t_table, t_table],
        out_specs=[tposed, tile, v_tposed, pl.BlockSpec((1, nq, 1, d), lambda b, t: (b, t, 0, 0))],
        out_shape=[jax.ShapeDtypeStruct((bsz, nb, d, MOBA_BLOCK), _BF16),
                   jax.ShapeDtypeStruct((bsz, s, d), _BF16),
                   jax.ShapeDtypeStruct((bsz, nb, d_aug, MOBA_BLOCK), _BF16),
                   jax.ShapeDtypeStruct((bsz, nb, 1, d), _F32)],
        compiler_params=pltpu.CompilerParams(
            dimension_semantics=("arbitrary", "arbitrary"), vmem_limit_bytes=VMEM_LIMIT),
        name="qkv_rope",
    )(x, g.reshape(1, d), wq.T, wk, wv.T, cos_k, sin_k, cos_t, sin_t)


def _moba_kernel(qt_ref, k_ref, vt_ref, kmean_ref, causal_ref, o_ref, q_ref, bias_ref, acc_ref, *, head_dim):
    n_tiles = qt_ref.shape[1]
    streams = [_attend_tile(pl.program_id(2) * n_tiles + tile, tile, qt_ref, k_ref, vt_ref, kmean_ref, causal_ref,
                            o_ref, q_ref.at[tile], bias_ref.at[tile], acc_ref.at[tile], head_dim=head_dim)
               for tile in range(n_tiles)]
    for stream in streams:
        stream()


def _attend_tile(own, tile, qt_ref, k_ref, vt_ref, kmean_ref, causal_ref, o_ref, q_ref, bias_ref, acc_ref, *, head_dim):
    nb = kmean_ref.shape[1]
    blk = MOBA_BLOCK
    width = HEADS_PER_STEP * blk
    v_rows = head_dim + ONES_ROWS

    def head_cols(hh):
        return slice(hh * blk, (hh + 1) * blk)

    def k_block(j):
        return k_ref[0, pl.ds(pl.multiple_of(j * blk, blk), blk), :]

    def v_block(j, hh):
        return vt_ref[0, j, hh * v_rows:(hh + 1) * v_rows, :]

    q_ref[...] = jnp.zeros_like(q_ref)
    for hh in range(HEADS_PER_STEP):
        feats = slice(hh * head_dim, (hh + 1) * head_dim)
        q_ref[feats, head_cols(hh)] = qt_ref[0, tile, feats, :]
    q_all = q_ref[...]

    block_id = lax.broadcasted_iota(jnp.int32, (nb, width), 0)
    gate = jnp.where(block_id < own, _dot(kmean_ref[0].astype(_BF16), q_all), NEG_INF)
    bias = jnp.full((nb, width), NEG_INF, _F32)
    for _ in range(MOBA_TOPK):
        best = jnp.max(gate, axis=0, keepdims=True)
        first = jnp.min(jnp.where(gate == best, block_id, nb), axis=0, keepdims=True)
        picked = block_id == first
        bias = jnp.where(picked, 0.0, bias)
        gate = jnp.where(picked, BELOW_NEG_INF, gate)
    bias_ref[...] = jnp.where(block_id < own, bias, NEG_INF)

    s_own = _dot(k_block(own), q_all) + causal_ref[...]
    own_peak = jnp.max(s_own.reshape(blk // 8, 8, width), axis=0)
    p_own = jnp.exp2(s_own).astype(_BF16)
    for hh in range(HEADS_PER_STEP):
        acc_ref[hh] = _dot(v_block(own, hh), p_own[:, head_cols(hh)])

    def stream():
        def visit_groups(first_block, group, n_groups, peak):
            def visit(i, peak):
                j0 = first_block + i * group
                keys = k_ref[0, pl.ds(pl.multiple_of(j0 * blk, blk), group * blk), :]
                s = _dot(keys, q_ref[...])
                updates = [None] * HEADS_PER_STEP
                for g in range(group):
                    t = s[g * blk:(g + 1) * blk] + bias_ref[pl.ds(j0 + g, 1), :]
                    peak = jnp.maximum(peak, jnp.max(t.reshape(blk // 8, 8, width), axis=0))
                    p = jnp.exp2(t).astype(_BF16)
                    for hh in range(HEADS_PER_STEP):
                        u = _dot(v_block(j0 + g, hh), p[:, head_cols(hh)])
                        updates[hh] = u if updates[hh] is None else updates[hh] + u
                for hh in range(HEADS_PER_STEP):
                    acc_ref[hh] += updates[hh]
                return peak

            return lax.fori_loop(0, n_groups, visit, peak)

        peak, first, left = own_peak, 0, own
        for size in GROUP_SIZES[:-1]:
            n_groups = left // size
            peak = visit_groups(first, size, n_groups, peak)
            first, left = first + n_groups * size, left - n_groups * size
        peak = visit_groups(first, GROUP_SIZES[-1], (left + GROUP_SIZES[-1] - 1) // GROUP_SIZES[-1], peak)
        query_peak = jnp.max(peak, axis=0, keepdims=True)
        out_of_range = (jnp.max(query_peak) > EXPONENT_GUARD) | (jnp.min(query_peak) < -EXPONENT_GUARD)

        @pl.when(out_of_range)
        def _():
            for hh in range(HEADS_PER_STEP):
                s = _dot(k_block(own), q_ref[:, head_cols(hh)]) + causal_ref[:, head_cols(hh)]
                m_own = jnp.max(s, axis=0, keepdims=True)
                acc = _dot(v_block(own, hh), jnp.exp2(s - m_own).astype(_BF16))

                def visit_exact(j, carry, hh=hh):
                    m_run, acc = carry
                    t = _dot(k_block(j), q_ref[:, head_cols(hh)]) + bias_ref[pl.ds(j, 1), head_cols(hh)]
                    m_new = jnp.maximum(m_run, jnp.max(t, axis=0, keepdims=True))
                    acc = jnp.exp2(m_run - m_new) * acc + _dot(v_block(j, hh), jnp.exp2(t - m_new).astype(_BF16))
                    return m_new, acc

                _, acc = lax.fori_loop(0, own, visit_exact, (m_own, acc))
                acc_ref[hh] = acc

        o_t = jnp.concatenate([acc_ref[hh, :head_dim] / acc_ref[hh, head_dim:head_dim + 1]
                               for hh in range(HEADS_PER_STEP)], axis=0)
        o_ref[0, tile * blk:(tile + 1) * blk] = o_t.T.astype(_BF16)

    return stream


def _moba(qt, k, vt, kmean):
    bsz, s, d = k.shape
    head_dim = d // N_HEADS
    nb = s // MOBA_BLOCK
    width = HEADS_PER_STEP * head_dim
    v_rows = head_dim + ONES_ROWS
    pos = jnp.arange(MOBA_BLOCK)
    causal = jnp.tile(jnp.where(pos[:, None] <= pos[None, :], 0.0, NEG_INF).astype(_F32), (1, HEADS_PER_STEP))
    return pl.pallas_call(
        functools.partial(_moba_kernel, head_dim=head_dim),
        grid=(bsz, d // width, nb // ATTN_TILES),
        in_specs=[pl.BlockSpec((1, ATTN_TILES, width, MOBA_BLOCK), lambda b, c, i: (b, i, c, 0)),
                  pl.BlockSpec((1, s, width), lambda b, c, i: (b, 0, c)),
                  pl.BlockSpec((1, nb, HEADS_PER_STEP * v_rows, MOBA_BLOCK), lambda b, c, i: (b, 0, c, 0)),
                  pl.BlockSpec((1, nb, width), lambda b, c, i: (b, 0, c)),
                  pl.BlockSpec(causal.shape, lambda b, c, i: (0, 0))],
        out_specs=pl.BlockSpec((1, ATTN_TILES * MOBA_BLOCK, width), lambda b, c, i: (b, i, c)),
        out_shape=jax.ShapeDtypeStruct((bsz, s, d), _BF16),
        scratch_shapes=[pltpu.VMEM((ATTN_TILES, width, HEADS_PER_STEP * MOBA_BLOCK), _BF16),
                        pltpu.VMEM((ATTN_TILES, nb, HEADS_PER_STEP * MOBA_BLOCK), _F32),
                        pltpu.VMEM((ATTN_TILES, HEADS_PER_STEP, v_rows, MOBA_BLOCK), _F32)],
        compiler_params=pltpu.CompilerParams(
            dimension_semantics=("arbitrary", "arbitrary", "arbitrary"), vmem_limit_bytes=VMEM_LIMIT),
        name="moba_attention",
    )(qt, k, vt, kmean.reshape(bsz, nb, d), causal)


def kernel(x, mix_norm, sc_w_in, sc_w_conv, sc_w_out, moba_w_qkv, moba_w_o, ffn_norm, ffn_w_up, ffn_w_conv,
           ffn_w_down, final_norm):
    bsz, s, d = x.shape
    assert N_HEADS % HEADS_PER_STEP == 0 and (d // N_HEADS * HEADS_PER_STEP) % LANES == 0
    assert s % (MOBA_BLOCK * ATTN_TILES) == 0 and s % (MOBA_BLOCK * QKV_BLOCKS) == 0
    assert s % min(TM_FFN, s) == 0 and s % min(TM_MIXER, s) == 0
    assert mix_norm.shape[0] == 2 and ffn_w_down.shape[1] % CK_FFN == 0 and d % CK_MIXER == 0

    x = _mixer(x, mix_norm[0], sc_w_in[0], sc_w_conv[0], sc_w_out[0])
    x = _ffn(x, ffn_norm[0], ffn_w_up[0], ffn_w_conv[0], ffn_w_down[0])
    qt, k, vt, kmean = _qkv(x, mix_norm[1], moba_w_qkv[0], _rope_tables(s, d // N_HEADS))
    attn = _moba(qt, k, vt, kmean)
    return _ffn(x, ffn_norm[1], ffn_w_up[1], ffn_w_conv[1], ffn_w_down[1],
                attn=attn, w_o=moba_w_o[0], final_g=final_norm)
```
